```python
import math
import jax
import jax.numpy as jnp
from jax import lax
import numpy as np

D_MODEL = 4096
BATCH = 4
SEQ = 2048
DEPTH = 2
DEC_BATCH = 8
DEC_SEQ = 1
PAST_LEN = 16384
PAGE_SIZE = 128

D_FF = ((8 * D_MODEL // 3 + 255) // 256) * 256
W_MIX = D_MODEL // 4
N_BRANCH = 4
W_CONV = W_MIX
CONV_WIDTH = 31
W_SGU = W_MIX
SGU_CHUNK = 128
SGU_GROUPS = 8
SGU_GC = W_SGU // SGU_GROUPS
HEAD_DIM = 128
NSA_HEADS = W_MIX // HEAD_DIM
NSA_KV_HEADS = 2
NSA_GROUP = NSA_HEADS // NSA_KV_HEADS
CMP_STRIDE = 16
CMP_SEGS = 2
CMP_LEN = CMP_STRIDE * CMP_SEGS
SLC_LEN = 64
N_SELECT = 16
WINDOW = 512
Q_BLOCK = 128
FORCE_BONUS = 1.0e4
GLA_HEADS = 8
GLA_DV = W_MIX // GLA_HEADS
GLA_DK = GLA_DV // 2
GLA_RANK = 16
GLA_TAU = 16.0
GLA_CHUNK = 64
REL_BUCKETS = 32
REL_MAX_DIST = 128
PLE_DIM = 256
EPS = 1e-6

IN_SPLITS = (
    2 * W_CONV,
    2 * W_SGU,
    NSA_HEADS * HEAD_DIM,
    2 * NSA_KV_HEADS * HEAD_DIM,
    2 * NSA_KV_HEADS * HEAD_DIM,
    2 * NSA_KV_HEADS * HEAD_DIM,
    3 * NSA_HEADS,
    GLA_HEADS * GLA_DK,
    GLA_HEADS * GLA_DK,
    GLA_HEADS * GLA_DV,
    GLA_RANK,
    GLA_HEADS * GLA_DV,
    N_BRANCH * D_MODEL,
)
N_IN = sum(IN_SPLITS)

kernel_name = 'hybrid_gated_conv_sgu_nsa_gla_decoder_step'


def _rms(x, g):
    x32 = x.astype(jnp.float32)
    y = x32 * lax.rsqrt(jnp.mean(x32 * x32, axis=-1, keepdims=True) + EPS)
    return (y * g.astype(jnp.float32)).astype(x.dtype)


def _layernorm(x, g, b):
    x32 = x.astype(jnp.float32)
    mu = jnp.mean(x32, axis=-1, keepdims=True)
    var = jnp.mean(jnp.square(x32 - mu), axis=-1, keepdims=True)
    y = (x32 - mu) * lax.rsqrt(var + EPS) * g.astype(jnp.float32) + b.astype(jnp.float32)
    return y.astype(x.dtype)


def _swiglu(x, w_in, w_out):
    gate, up = jnp.split(x @ w_in, 2, axis=-1)
    return (jax.nn.silu(gate) * up) @ w_out


def _split_in(z):
    points = np.cumsum(IN_SPLITS)[:-1].tolist()
    return jnp.split(z, points, axis=-1)


def _rel_bucket(dist):
    n = jnp.maximum(dist, 0)
    max_exact = REL_BUCKETS // 2
    nf = jnp.maximum(n, 1).astype(jnp.float32)
    large = max_exact + (jnp.log(nf / max_exact) / math.log(REL_MAX_DIST / max_exact)
                         * (REL_BUCKETS - max_exact)).astype(jnp.int32)
    large = jnp.minimum(large, REL_BUCKETS - 1)
    return jnp.where(n < max_exact, n, large)


def _masked_softmax(s, mask):
    s = jnp.where(mask, s.astype(jnp.float32), -jnp.inf)
    m = jnp.max(s, axis=-1, keepdims=True)
    m = jnp.where(jnp.isfinite(m), m, 0.0)
    p = jnp.exp(s - m)
    return p / jnp.maximum(jnp.sum(p, axis=-1, keepdims=True), 1e-30)


def _attend_shared(q, k, v, q_pos, k_pos, mask, rel_b):
    s = jnp.einsum('bqhgd,bkhd->bhgqk', q, k).astype(jnp.float32)
    bias = rel_b[_rel_bucket(q_pos[:, None] - k_pos[None, :])]
    s = s + jnp.transpose(bias, (2, 3, 0, 1)).astype(jnp.float32)
    p = _masked_softmax(s, mask)
    o = jnp.einsum('bhgqk,bkhd->bqhgd', p.astype(v.dtype), v)
    return o, p


def _compress(kv, w_cmp):
    B, L = kv.shape[:2]
    n_seg = L // CMP_STRIDE
    seg = kv[:, :n_seg * CMP_STRIDE].reshape(B, n_seg, CMP_STRIDE, 2, NSA_KV_HEADS, HEAD_DIM)
    w = w_cmp.reshape(2, CMP_SEGS, CMP_STRIDE, HEAD_DIM, HEAD_DIM)
    proj = jnp.einsum('bnlchd,cmlde->mbnche', seg, w)
    n_cmp = n_seg - CMP_SEGS + 1
    out = proj[0, :, :n_cmp]
    for m in range(1, CMP_SEGS):
        out = out + proj[m, :, m:m + n_cmp]
    return out


def _nsa_compressed(q, q_pos, kv_rows, w_cmp, rel_b):
    L = kv_rows.shape[1]
    cb = _compress(kv_rows, w_cmp)
    n_cmp = cb.shape[1]
    c_start = jnp.arange(n_cmp) * CMP_STRIDE
    blk_end = c_start + CMP_LEN - 1
    mask = blk_end[None, :] <= q_pos[:, None]
    o, p = _attend_shared(q, cb[:, :, 0], cb[:, :, 1], q_pos, blk_end, mask, rel_b)
    n_slc = -(-L // SLC_LEN)
    s_start = jnp.arange(n_slc) * SLC_LEN
    cover = ((c_start[:, None] < s_start[None, :] + SLC_LEN)
             & (c_start[:, None] + CMP_LEN > s_start[None, :])).astype(jnp.float32)
    imp = jnp.einsum('bhgqk,kj->bqhj', p, cover)
    return o, imp


def _select_blocks(imp, q_pos):
    n_slc = imp.shape[-1]
    blk = jnp.arange(n_slc)[None, None, :]
    qp = q_pos[:, None, None]
    cur = qp // SLC_LEN
    valid = blk * SLC_LEN <= qp
    forced = (blk == 0) | (blk == cur) | (blk == cur - 1)
    score = jnp.where(valid, imp + FORCE_BONUS * forced, -jnp.inf)
    top, idx = lax.top_k(score, min(N_SELECT, n_slc))
    return idx, jnp.isfinite(top)


def _nsa_selected(q, q_pos, idx, ok, fetch, rel_b):
    tok = (idx[..., None] * SLC_LEN + jnp.arange(SLC_LEN)).reshape(idx.shape[:-1] + (-1,))
    qp = q_pos[None, :, None, None]
    tok_ok = jnp.repeat(ok, SLC_LEN, axis=-1) & (tok <= qp)
    k, v = fetch(tok)
    s = jnp.einsum('bqhgd,bqhnd->bqhgn', q, k).astype(jnp.float32)
    hi = jnp.arange(NSA_KV_HEADS)[None, None, :, None]
    bias = rel_b[_rel_bucket(qp - tok), hi]
    s = s + jnp.swapaxes(bias, -1, -2).astype(jnp.float32)
    p = _masked_softmax(s, tok_ok[:, :, :, None, :])
    return jnp.einsum('bqhgn,bqhnd->bqhgd', p.astype(v.dtype), v)


def _nsa_window(q, q_pos, kv, k_pos, rel_b):
    d = q_pos[:, None] - k_pos[None, :]
    mask = (k_pos[None, :] >= 0) & (d >= 0) & (d <= WINDOW)
    o, _ = _attend_shared(q, kv[:, :, 0], kv[:, :, 1], q_pos, k_pos, mask, rel_b)
    return o


def _nsa_prompt(q, kv_cmp, kv_slc, kv_win, w_cmp, rel_b):
    B, T = q.shape[:2]
    pos = jnp.arange(T)
    o_cmp, imp = _nsa_compressed(q, pos, kv_cmp, w_cmp, rel_b)
    win_pad = jnp.pad(kv_win, ((0, 0), (WINDOW, 0), (0, 0), (0, 0), (0, 0)))
    band = WINDOW + Q_BLOCK
    ks, vs = kv_slc[:, :, 0], kv_slc[:, :, 1]
    bi = jnp.arange(B)[:, None, None, None]
    hi = jnp.arange(NSA_KV_HEADS)[None, None, :, None]

    def fetch(tok):
        t = jnp.clip(tok, 0, T - 1)
        return ks[bi, t, hi], vs[bi, t, hi]

    def block(i):
        start = i * Q_BLOCK
        qb = lax.dynamic_slice_in_dim(q, start, Q_BLOCK, 1)
        pb = start + jnp.arange(Q_BLOCK)
        ib = lax.dynamic_slice_in_dim(imp, start, Q_BLOCK, 1)
        idx, ok = _select_blocks(ib, pb)
        o_s = _nsa_selected(qb, pb, idx, ok, fetch, rel_b)
        wb = lax.dynamic_slice_in_dim(win_pad, start, band, 1)
        o_w = _nsa_window(qb, pb, wb, start - WINDOW + jnp.arange(band), rel_b)
        return o_s, o_w

    o_slc, o_win = lax.map(block, jnp.arange(T // Q_BLOCK))
    o_slc = jnp.moveaxis(o_slc, 0, 1).reshape(q.shape)
    o_win = jnp.moveaxis(o_win, 0, 1).reshape(q.shape)
    return o_cmp, o_slc, o_win


def _nsa_sample(q, kv_cmp, kv_slc, kv_win, cache_cmp, cache_slc, win_buf, page_table, w_cmp, rel_b):
    B, Tn = q.shape[:2]
    past = page_table.shape[1] * PAGE_SIZE
    pos = past + jnp.arange(Tn)
    past_cmp = cache_cmp[page_table].reshape(B, past, 2, NSA_KV_HEADS, HEAD_DIM)
    cmp_rows = jnp.concatenate([past_cmp.astype(kv_cmp.dtype), kv_cmp], axis=1)
    o_cmp, imp = _nsa_compressed(q, pos, cmp_rows, w_cmp, rel_b)
    idx, ok = _select_blocks(imp, pos)
    bi = jnp.arange(B)[:, None, None, None]
    hi = jnp.arange(NSA_KV_HEADS)[None, None, :, None]

    def fetch(tok):
        t_past = jnp.clip(tok, 0, past - 1)
        phys = page_table[bi, t_past // PAGE_SIZE]
        off = t_past % PAGE_SIZE
        t_new = jnp.clip(tok - past, 0, Tn - 1)
        is_past = (tok < past)[..., None]
        k = jnp.where(is_past, cache_slc[phys, off, 0, hi].astype(kv_slc.dtype), kv_slc[bi, t_new, 0, hi])
        v = jnp.where(is_past, cache_slc[phys, off, 1, hi].astype(kv_slc.dtype), kv_slc[bi, t_new, 1, hi])
        return k, v

    o_slc = _nsa_selected(q, pos, idx, ok, fetch, rel_b)
    nbuf = win_buf.shape[1]
    wrows = jnp.concatenate([win_buf.astype(kv_win.dtype), kv_win], axis=1)
    o_win = _nsa_window(q, pos, wrows, past - nbuf + jnp.arange(wrows.shape[1]), rel_b)
    return o_cmp, o_slc, o_win, wrows[:, wrows.shape[1] - nbuf:]


def _gla(q, k, v, log_a, s0):
    B, T, H, dk = q.shape
    dv = v.shape[-1]
    C = min(GLA_CHUNK, T)
    n = -(-T // C)
    pad = n * C - T

    def chunks(a):
        a = jnp.pad(a.astype(jnp.float32), ((0, 0), (0, pad), (0, 0), (0, 0)))
        return a.reshape(B, n, C, H, a.shape[-1])

    qc, kc, vc, ac = chunks(q), chunks(k), chunks(v), chunks(log_a)
    b = jnp.cumsum(ac, axis=2)
    q_in = qc * jnp.exp(b)
    k_in = kc * jnp.exp(-b)
    k_end = kc * jnp.exp(b[:, :, -1:] - b)
    causal = jnp.tril(jnp.ones((C, C), dtype=bool))
    a_intra = jnp.where(causal, jnp.einsum('bnthk,bnshk->bnhts', q_in, k_in), 0.0)
    o_intra = jnp.einsum('bnhts,bnshv->bnthv', a_intra, vc)

    def step(S, xs):
        qi, ke, vi, bl = xs
        o = jnp.einsum('bthk,bhkv->bthv', qi, S)
        S = jnp.exp(bl)[..., None] * S + jnp.einsum('bshk,bshv->bhkv', ke, vi)
        return S, o

    xs = (jnp.moveaxis(q_in, 1, 0), jnp.moveaxis(k_end, 1, 0), jnp.moveaxis(vc, 1, 0),
          jnp.moveaxis(b[:, :, -1], 1, 0))
    S, o_inter = lax.scan(step, s0.astype(jnp.float32), xs)
    o = o_intra + jnp.moveaxis(o_inter, 0, 1)
    return o.reshape(B, n * C, H, dv)[:, :T], S


def _conv_module(a, buf, conv_w, conv_b, ln_g, ln_b, w_out):
    xp = jnp.concatenate([buf.astype(a.dtype), a], axis=1)
    y = lax.conv_general_dilated(xp, conv_w[:, None, :].astype(a.dtype), window_strides=(1,),
                                 padding='VALID', dimension_numbers=('NWC', 'WIO', 'NWC'),
                                 feature_group_count=a.shape[-1]) + conv_b
    y = jax.nn.silu(_layernorm(y, ln_g, ln_b))
    return y @ w_out, xp[:, xp.shape[1] - (CONV_WIDTH - 1):]


def _sgu(u, v, ws, bs):
    B, T, _ = v.shape
    n = -(-T // SGU_CHUNK)
    pad = n * SGU_CHUNK - T
    vc = jnp.pad(v, ((0, 0), (0, pad), (0, 0))).reshape(B, n, SGU_CHUNK, SGU_GROUPS, SGU_GC)
    w_causal = ws * jnp.tril(jnp.ones((SGU_CHUNK, SGU_CHUNK), ws.dtype))
    mix = jnp.einsum('gts,bnsgc->bntgc', w_causal, vc) + bs.T[None, None, :, :, None]
    return u * mix.reshape(B, n * SGU_CHUNK, W_SGU)[:, :T]


def _layer(x, ple, w, rel_b, st):
    B, T, _ = x.shape
    x = x + 0.5 * _swiglu(_rms(x, w['ffn1_norm']), w['ffn1_w_in'], w['ffn1_w_out'])
    h = _rms(x, w['mix_norm'])
    (a_in, b_in, c_q, c_cmp, c_slc, c_win, c_gate,
     d_q, d_k, d_v, d_a, d_r, g_merge) = _split_in(h @ w['w_in'])
    new = {}

    a = a_in[..., :W_CONV] * jax.nn.sigmoid(a_in[..., W_CONV:])
    conv_buf = jnp.zeros((B, CONV_WIDTH - 1, W_CONV), a.dtype) if st is None else st['conv']
    y_a, new['conv'] = _conv_module(a, conv_buf, w['conv_w'], w['conv_b'], w['conv_ln_g'],
                                    w['conv_ln_b'], w['conv_w_out'])

    u, v = jnp.split(jax.nn.gelu(b_in), 2, axis=-1)
    v = _layernorm(v, w['sgu_ln_g'], w['sgu_ln_b'])
    y_b = _sgu(u, v, w['sgu_ws'], w['sgu_bs']) @ w['sgu_w_out']

    kv_shape = (B, T, 2, NSA_KV_HEADS, HEAD_DIM)
    q = (c_q * HEAD_DIM ** -0.5).reshape(B, T, NSA_KV_HEADS, NSA_GROUP, HEAD_DIM)
    kv_cmp, kv_slc, kv_win = c_cmp.reshape(kv_shape), c_slc.reshape(kv_shape), c_win.reshape(kv_shape)
    if st is None:
        o_cmp, o_slc, o_win = _nsa_prompt(q, kv_cmp, kv_slc, kv_win, w['nsa_w_cmp'], rel_b)
        new['win'] = kv_win[:, max(T - WINDOW, 0):]
        new['cmp'] = kv_cmp.reshape(B, T // PAGE_SIZE, PAGE_SIZE, 2, NSA_KV_HEADS, HEAD_DIM)
        new['slc'] = kv_slc.reshape(B, T // PAGE_SIZE, PAGE_SIZE, 2, NSA_KV_HEADS, HEAD_DIM)
    else:
        o_cmp, o_slc, o_win, new['win'] = _nsa_sample(q, kv_cmp, kv_slc, kv_win, st['cmp'], st['slc'],
                                                      st['win'], st['page_table'], w['nsa_w_cmp'], rel_b)
        new['cmp'] = kv_cmp
        new['slc'] = kv_slc
        new['sgu_v'] = v
    gc = jax.nn.sigmoid(c_gate.astype(jnp.float32)).reshape(B, T, NSA_KV_HEADS, NSA_GROUP, 3)
    o_c = gc[..., 0:1] * o_cmp + gc[..., 1:2] * o_slc + gc[..., 2:3] * o_win
    y_c = o_c.astype(x.dtype).reshape(B, T, NSA_HEADS * HEAD_DIM) @ w['nsa_w_out']

    dq = (d_q * GLA_DK ** -0.5).reshape(B, T, GLA_HEADS, GLA_DK)
    dk = d_k.reshape(B, T, GLA_HEADS, GLA_DK)
    dv = d_v.reshape(B, T, GLA_HEADS, GLA_DV)
    log_a = jax.nn.log_sigmoid((d_a @ w['gla_w_a2'] + w['gla_b_a']).astype(jnp.float32))
    log_a = log_a.reshape(B, T, GLA_HEADS, GLA_DK) / GLA_TAU
    s0 = jnp.zeros((B, GLA_HEADS, GLA_DK, GLA_DV), jnp.float32) if st is None else st['gla']
    o_d, new['gla'] = _gla(dq, dk, dv, log_a, s0)
    o_d = _rms(o_d, w['gla_norm']).astype(x.dtype).reshape(B, T, W_MIX) * jax.nn.silu(d_r)
    y_d = o_d @ w['gla_w_out']

    g = jax.nn.sigmoid(g_merge).reshape(B, T, N_BRANCH, D_MODEL)
    merged = g[:, :, 0] * y_a + g[:, :, 1] * y_b + g[:, :, 2] * y_c + g[:, :, 3] * y_d
    x = x + merged @ w['w_out']

    x = x + 0.5 * _swiglu(_rms(x, w['ffn2_norm']), w['ffn2_w_in'], w['ffn2_w_out'])
    x = x + jax.nn.sigmoid(_rms(x, w['pe_norm']) @ w['w_pe_gate']) * (ple @ w['w_pe'])
    return x, new


def setup_inputs(seed: int = 0) -> dict:
    key = jax.random.key(seed)
    keys = iter(jax.random.split(key, 64))

    def nrm(shape, scale):
        return jax.random.normal(next(keys), shape, jnp.float32) * scale

    def gain(shape):
        return 1.0 + nrm(shape, 0.02)

    n_pages = PAST_LEN // PAGE_SIZE
    n_used = DEC_BATCH * n_pages
    n_pool = n_used + max(1, n_used // 4)
    win_buf = min(WINDOW, PAST_LEN)
    page_table = jax.random.permutation(next(keys), n_pool)[:n_used].reshape(DEC_BATCH, n_pages).astype(jnp.int32)
    L = DEPTH
    return {
        'x_prompt': nrm((BATCH, SEQ, D_MODEL), 1.0),
        'x_sample': nrm((DEC_BATCH, DEC_SEQ, D_MODEL), 1.0),
        'cache_cmp_kv': nrm((L, n_pool, PAGE_SIZE, 2, NSA_KV_HEADS, HEAD_DIM), 1.0),
        'cache_slc_kv': nrm((L, n_pool, PAGE_SIZE, 2, NSA_KV_HEADS, HEAD_DIM), 1.0),
        'state_win_kv': nrm((L, DEC_BATCH, win_buf, 2, NSA_KV_HEADS, HEAD_DIM), 1.0),
        'state_conv': nrm((L, DEC_BATCH, CONV_WIDTH - 1, W_CONV), 0.5),
        'state_gla': nrm((L, DEC_BATCH, GLA_HEADS, GLA_DK, GLA_DV), 0.5),
        'page_table': page_table,
        'p_prompt': nrm((L, BATCH, SEQ, PLE_DIM), 1.0),
        'p_sample': nrm((L, DEC_BATCH, DEC_SEQ, PLE_DIM), 1.0),
        'ffn1_norm': gain((L, D_MODEL)),
        'ffn1_w_in': nrm((L, D_MODEL, 2 * D_FF), D_MODEL ** -0.5),
        'ffn1_w_out': nrm((L, D_FF, D_MODEL), D_FF ** -0.5),
        'mix_norm': gain((L, D_MODEL)),
        'w_in': nrm((L, D_MODEL, N_IN), D_MODEL ** -0.5),
        'conv_w': nrm((L, CONV_WIDTH, W_CONV), CONV_WIDTH ** -0.5),
        'conv_b': nrm((L, W_CONV), 0.02),
        'conv_ln_g': gain((L, W_CONV)),
        'conv_ln_b': nrm((L, W_CONV), 0.02),
        'conv_w_out': nrm((L, W_CONV, D_MODEL), W_CONV ** -0.5),
        'sgu_ln_g': gain((L, W_SGU)),
        'sgu_ln_b': nrm((L, W_SGU), 0.02),
        'sgu_ws': nrm((L, SGU_GROUPS, SGU_CHUNK, SGU_CHUNK), SGU_CHUNK ** -0.5),
        'sgu_bs': 1.0 + nrm((L, SGU_GROUPS, SGU_CHUNK), 0.02),
        'sgu_w_out': nrm((L, W_SGU, D_MODEL), W_SGU ** -0.5),
        'nsa_w_cmp': nrm((L, 2, CMP_LEN, HEAD_DIM, HEAD_DIM), (CMP_LEN * HEAD_DIM) ** -0.5),
        'nsa_w_out': nrm((L, NSA_HEADS * HEAD_DIM, D_MODEL), (NSA_HEADS * HEAD_DIM) ** -0.5),
        'gla_w_a2': nrm((L, GLA_RANK, GLA_HEADS * GLA_DK), GLA_RANK ** -0.5),
        'gla_b_a': nrm((L, GLA_HEADS * GLA_DK), 0.1),
        'gla_norm': gain((L, GLA_DV)),
        'gla_w_out': nrm((L, W_MIX, D_MODEL), W_MIX ** -0.5),
        'w_out': nrm((L, D_MODEL, D_MODEL), D_MODEL ** -0.5),
        'ffn2_norm': gain((L, D_MODEL)),
        'ffn2_w_in': nrm((L, D_MODEL, 2 * D_FF), D_MODEL ** -0.5),
        'ffn2_w_out': nrm((L, D_FF, D_MODEL), D_FF ** -0.5),
        'pe_norm': gain((L, D_MODEL)),
        'w_pe': nrm((L, PLE_DIM, D_MODEL), PLE_DIM ** -0.5),
        'w_pe_gate': nrm((L, D_MODEL, D_MODEL), D_MODEL ** -0.5),
        'rel_bias': nrm((REL_BUCKETS, NSA_HEADS), 0.2),
        'final_norm': gain((D_MODEL,)),
    }


def reference(x_prompt, x_sample, cache_cmp_kv, cache_slc_kv, state_win_kv, state_conv, state_gla,
              page_table, p_prompt, p_sample, ffn1_norm, ffn1_w_in, ffn1_w_out, mix_norm, w_in,
              conv_w, conv_b, conv_ln_g, conv_ln_b, conv_w_out, sgu_ln_g, sgu_ln_b, sgu_ws, sgu_bs,
              sgu_w_out, nsa_w_cmp, nsa_w_out, gla_w_a2, gla_b_a, gla_norm, gla_w_out, w_out,
              ffn2_norm, ffn2_w_in, ffn2_w_out, pe_norm, w_pe, w_pe_gate, rel_bias, final_norm):
    rel_b = rel_bias.reshape(REL_BUCKETS, NSA_KV_HEADS, NSA_GROUP)
    xp, xs = x_prompt, x_sample
    sp_all, ss_all = [], []
    for i in range(DEPTH):
        w = {
            'ffn1_norm': ffn1_norm[i], 'ffn1_w_in': ffn1_w_in[i], 'ffn1_w_out': ffn1_w_out[i],
            'mix_norm': mix_norm[i], 'w_in': w_in[i],
            'conv_w': conv_w[i], 'conv_b': conv_b[i], 'conv_ln_g': conv_ln_g[i], 'conv_ln_b': conv_ln_b[i],
            'conv_w_out': conv_w_out[i],
            'sgu_ln_g': sgu_ln_g[i], 'sgu_ln_b': sgu_ln_b[i], 'sgu_ws': sgu_ws[i], 'sgu_bs': sgu_bs[i],
            'sgu_w_out': sgu_w_out[i],
            'nsa_w_cmp': nsa_w_cmp[i], 'nsa_w_out': nsa_w_out[i],
            'gla_w_a2': gla_w_a2[i], 'gla_b_a': gla_b_a[i], 'gla_norm': gla_norm[i], 'gla_w_out': gla_w_out[i],
            'w_out': w_out[i],
            'ffn2_norm': ffn2_norm[i], 'ffn2_w_in': ffn2_w_in[i], 'ffn2_w_out': ffn2_w_out[i],
            'pe_norm': pe_norm[i], 'w_pe': w_pe[i], 'w_pe_gate': w_pe_gate[i],
        }
        xp, sp = _layer(xp, p_prompt[i], w, rel_b, None)
        st = {'cmp': cache_cmp_kv[i], 'slc': cache_slc_kv[i], 'win': state_win_kv[i],
              'conv': state_conv[i], 'gla': state_gla[i], 'page_table': page_table}
        xs, ss = _layer(xs, p_sample[i], w, rel_b, st)
        sp_all.append(sp)
        ss_all.append(ss)

    def stack(lst, name):
        return jnp.stack([d[name] for d in lst], axis=0)

    y_prompt = _rms(xp, final_norm)
    y_sample = _rms(xs, final_norm)
    return (y_prompt, y_sample,
            stack(sp_all, 'cmp'), stack(ss_all, 'cmp'),
            stack(sp_all, 'slc'), stack(ss_all, 'slc'),
            stack(sp_all, 'win'), stack(ss_all, 'win'),
            stack(sp_all, 'conv'), stack(ss_all, 'conv'),
            stack(sp_all, 'gla'), stack(ss_all, 'gla'),
            stack(ss_all, 'sgu_v'))
```

```python
import functools
import math

import jax
import jax.numpy as jnp
from jax import lax
from jax.experimental import pallas as pl
from jax.experimental.pallas import tpu as pltpu

F32 = jnp.float32
BF16 = jnp.bfloat16

D_MODEL = 4096
BATCH = 4
SEQ = 2048
DEPTH = 2
DEC_BATCH = 8
PAST_LEN = 16384
PAGE_SIZE = 128
D_FF = 11008
W_MIX = 1024
CONV_WIDTH = 31
SGU_CHUNK = 128
SGU_GROUPS = 8
HEAD_DIM = 128
NSA_HEADS = 8
KVH = 2
GRP = 4
CMP_STRIDE = 16
CMP_LEN = 32
SLC_LEN = 64
N_SELECT = 16
WINDOW = 512
Q_BLOCK = 128
FORCE_BONUS = 1.0e4
GLA_HEADS = 8
GLA_DV = 128
GLA_DK = 64
GLA_RANK = 16
GLA_TAU = 16.0
GLA_CHUNK = 64
REL_BUCKETS = 32
REL_MAX_DIST = 128
PLE_DIM = 256
EPS = 1e-6

LANES = 128
VMEM_LIMIT_BYTES = 56 * 1024 * 1024
M_PROMPT = BATCH * SEQ
M_SLAB = M_PROMPT + 16
TM = 912
TM_NORM = 304
FFN_CHUNK = 256
NEG = -1e30

C_VAL, C_GATE, C_U, C_V, C_Q, C_DV, C_DR, C_DQ, C_DK = 0, 1024, 2048, 3072, 4096, 5120, 6144, 7168, 8192
C_CMP, C_SLC, C_WIN, C_CG, C_DA = 9216, 9728, 10240, 10752, 10880
ZW = 11264
G_OFF = 9768

NT_DIMS = (((1,), (1,)), ((), ()))
TN_DIMS = (((0,), (0,)), ((), ()))


def _cp(n_axes, vmem=VMEM_LIMIT_BYTES):
    return pltpu.CompilerParams(dimension_semantics=("arbitrary",) * n_axes, vmem_limit_bytes=vmem)


def _dot(a, b):
    return jnp.dot(a, b, preferred_element_type=F32)


def _dot_nt(a, b):
    return lax.dot_general(a, b, NT_DIMS, preferred_element_type=F32)


def _dot_tn(a, b):
    return lax.dot_general(a, b, TN_DIMS, preferred_element_type=F32)


def _split_dot(x, w):
    hi = x.astype(BF16)
    lo = (x - hi.astype(F32)).astype(BF16)
    return _dot(hi, w) + _dot(lo, w)


def _silu(x):
    return x * jax.nn.sigmoid(x)


def _layernorm(x, g, b):
    mu = jnp.mean(x, axis=-1, keepdims=True)
    xc = x - mu
    var = jnp.mean(xc * xc, axis=-1, keepdims=True)
    return xc * lax.rsqrt(var + EPS) * g + b


def _masked_softmax(s, mask):
    s = jnp.where(mask, s, NEG)
    m = jnp.max(s, axis=-1, keepdims=True)
    p = jnp.where(mask, jnp.exp(s - m), 0.0)
    return p / jnp.maximum(jnp.sum(p, axis=-1, keepdims=True), 1e-30)


def _norm_kernel(*refs, has_delta, write_x):
    it = iter(refs)
    x_ref = next(it)
    d_ref = next(it) if has_delta else None
    g_ref = next(it)
    xo_ref = next(it) if write_x else None
    h_ref = next(it)
    x = x_ref[...]
    if has_delta:
        x = x + d_ref[...]
    if write_x:
        xo_ref[...] = x
    y = x * lax.rsqrt(jnp.mean(x * x, axis=-1, keepdims=True) + EPS) * g_ref[...]
    h_ref[...] = y.astype(h_ref.dtype)


def add_norm(x, delta, g, *, write_x=True, out_dtype=BF16):
    m, d = x.shape
    row = pl.BlockSpec((TM_NORM, d), lambda i: (i, 0))
    ins = [x] + ([delta] if delta is not None else []) + [g.reshape(1, d)]
    in_specs = [row] * (len(ins) - 1) + [pl.BlockSpec((1, d), lambda i: (0, 0))]
    out_shape = ([jax.ShapeDtypeStruct((m, d), F32)] if write_x else []) + [jax.ShapeDtypeStruct((m, d), out_dtype)]
    outs = pl.pallas_call(
        functools.partial(_norm_kernel, has_delta=delta is not None, write_x=write_x),
        grid=(m // TM_NORM,), in_specs=in_specs, out_specs=[row] * len(out_shape), out_shape=out_shape,
        compiler_params=_cp(1), name="add_norm")(*ins)
    return (outs[0], outs[1]) if write_x else (None, outs[0])


def _ffn_kernel(h_ref, wg_ref, wu_ref, wo_ref, o_ref):
    f = pl.program_id(1)
    h = h_ref[...]
    g = _dot(h, wg_ref[...])
    u = _dot(h, wu_ref[...])
    a = (0.5 * _silu(g) * u).astype(BF16)
    part = _dot(a, wo_ref[...])

    @pl.when(f == 0)
    def _():
        o_ref[...] = part

    @pl.when(f > 0)
    def _():
        o_ref[...] += part


def ffn_half(h, w_in, w_out):
    m, d = h.shape
    nf = D_FF // FFN_CHUNK
    once = pl.Buffered(1)
    return pl.pallas_call(
        _ffn_kernel, grid=(m // TM, nf),
        in_specs=[pl.BlockSpec((TM, d), lambda i, f: (i, 0), pipeline_mode=once),
                  pl.BlockSpec((d, FFN_CHUNK), lambda i, f: (0, f)),
                  pl.BlockSpec((d, FFN_CHUNK), lambda i, f: (0, f + nf)),
                  pl.BlockSpec((FFN_CHUNK, d), lambda i, f: (f, 0))],
        out_specs=pl.BlockSpec((TM, d), lambda i, f: (i, 0), pipeline_mode=once),
        out_shape=jax.ShapeDtypeStruct((m, d), F32), compiler_params=_cp(2), name="ffn")(h, w_in, w_in, w_out)


def _mm_kernel(x_ref, w_ref, o_ref, *, act):
    acc = _dot(x_ref[...], w_ref[...])
    if act == "sigmoid":
        acc = jax.nn.sigmoid(acc)
    o_ref[...] = acc.astype(o_ref.dtype)


def matmul(x, w, *, tn, act=None, out_dtype=F32):
    m, k = x.shape
    n = w.shape[1]
    return pl.pallas_call(
        functools.partial(_mm_kernel, act=act), grid=(n // tn, m // TM),
        in_specs=[pl.BlockSpec((TM, k), lambda j, i: (i, 0)), pl.BlockSpec((k, tn), lambda j, i: (0, j))],
        out_specs=pl.BlockSpec((TM, tn), lambda j, i: (i, j)),
        out_shape=jax.ShapeDtypeStruct((m, n), out_dtype), compiler_params=_cp(2), name="matmul")(x, w)


def _ple_kernel(h_ref, p_ref, wg_ref, wp_ref, o_ref):
    gate = jax.nn.sigmoid(_dot(h_ref[...], wg_ref[...]))
    o_ref[...] = gate * _dot(p_ref[...], wp_ref[...])


def ple_delta(h, ple, w_gate, w_pe, *, tn=1024):
    m, k = h.shape
    n = w_gate.shape[1]
    return pl.pallas_call(
        _ple_kernel, grid=(n // tn, m // TM),
        in_specs=[pl.BlockSpec((TM, k), lambda j, i: (i, 0)), pl.BlockSpec((TM, PLE_DIM), lambda j, i: (i, 0)),
                  pl.BlockSpec((k, tn), lambda j, i: (0, j)), pl.BlockSpec((PLE_DIM, tn), lambda j, i: (0, j))],
        out_specs=pl.BlockSpec((TM, tn), lambda j, i: (i, j)),
        out_shape=jax.ShapeDtypeStruct((m, n), F32), compiler_params=_cp(2), name="ple")(h, ple, w_gate, w_pe)


def _merge_kernel(*refs):
    o_refs, w_refs, g_refs, out_ref = refs[0:4], refs[4:8], refs[8:12], refs[12]
    acc = None
    for o_ref, w_ref, g_ref in zip(o_refs, w_refs, g_refs):
        term = g_ref[...].astype(F32) * _dot(o_ref[...], w_ref[...])
        acc = term if acc is None else acc + term
    out_ref[...] = acc.astype(out_ref.dtype)


def merge(branches, weights, gates, *, tn=512):
    m = branches[0].shape[0]
    nb = D_MODEL // tn
    in_specs = ([pl.BlockSpec((TM, W_MIX), lambda j, i: (i, 0))] * 4
                + [pl.BlockSpec((W_MIX, tn), lambda j, i: (0, j))] * 4
                + [pl.BlockSpec((TM, tn), lambda j, i, b=b: (i, b * nb + j)) for b in range(4)])
    return pl.pallas_call(
        _merge_kernel, grid=(nb, m // TM), in_specs=in_specs,
        out_specs=pl.BlockSpec((TM, tn), lambda j, i: (i, j)),
        out_shape=jax.ShapeDtypeStruct((m, D_MODEL), BF16), compiler_params=_cp(2), name="merge")(
            *branches, *weights, gates, gates, gates, gates)


CONV_HALO = 32


def _conv_kernel(val_ref, gate_ref, halo_ref, w_ref, cb_ref, lg_ref, lb_ref, o_ref, st_ref, aext, *, tt, rb, n_last):
    t = pl.program_id(1)

    @pl.when(t == 0)
    def _():
        aext[0:CONV_HALO, :] = halo_ref[...]

    aext[CONV_HALO:CONV_HALO + tt, :] = val_ref[...] * jax.nn.sigmoid(gate_ref[...])
    first = CONV_HALO - (CONV_WIDTH - 1)
    for r0 in range(0, tt, rb):
        acc = jnp.zeros((rb, W_MIX), F32) + cb_ref[...]
        for j in range(CONV_WIDTH):
            acc = acc + aext[r0 + first + j:r0 + first + j + rb, :] * w_ref[j:j + 1, :]
        y = _silu(_layernorm(acc, lg_ref[...], lb_ref[...]))
        o_ref[r0:r0 + rb, :] = y.astype(o_ref.dtype)

    @pl.when(t == pl.num_programs(1) - 1)
    def _():
        st_ref[...] = aext[n_last:n_last + CONV_HALO, :]

    if tt >= CONV_HALO:
        aext[0:CONV_HALO, :] = aext[tt:tt + CONV_HALO, :]


def conv_mixer(z, halo, conv_w, conv_b, ln_g, ln_b, *, nb, t_len, tt, rb, n_last):
    nt = t_len // tt
    w = jnp.zeros((CONV_HALO, W_MIX), F32).at[:CONV_WIDTH].set(conv_w)
    vec = pl.BlockSpec((1, W_MIX), lambda b, t: (0, 0))
    return pl.pallas_call(
        functools.partial(_conv_kernel, tt=tt, rb=rb, n_last=n_last), grid=(nb, nt),
        in_specs=[pl.BlockSpec((tt, W_MIX), lambda b, t: (b * nt + t, C_VAL // W_MIX)),
                  pl.BlockSpec((tt, W_MIX), lambda b, t: (b * nt + t, C_GATE // W_MIX)),
                  pl.BlockSpec((None, CONV_HALO, W_MIX), lambda b, t: (b, 0, 0)),
                  pl.BlockSpec((CONV_HALO, W_MIX), lambda b, t: (0, 0)), vec, vec, vec],
        out_specs=[pl.BlockSpec((tt, W_MIX), lambda b, t: (b * nt + t, 0)),
                   pl.BlockSpec((None, CONV_HALO, W_MIX), lambda b, t: (b, 0, 0))],
        out_shape=[jax.ShapeDtypeStruct((nb * t_len, W_MIX), BF16), jax.ShapeDtypeStruct((nb, CONV_HALO, W_MIX), F32)],
        scratch_shapes=[pltpu.VMEM((CONV_HALO + tt, W_MIX), F32)],
        compiler_params=_cp(2), name="conv")(z, z, halo, w, conv_b.reshape(1, -1), ln_g.reshape(1, -1), ln_b.reshape(1, -1))


def _sgu_kernel(u_ref, v_ref, wc_ref, bs_ref, lg_ref, lb_ref, o_ref, *, tt):
    for c0 in range(0, tt, SGU_CHUNK):
        rows = slice(c0, c0 + SGU_CHUNK)
        u = jax.nn.gelu(u_ref[rows, :])
        v = _layernorm(jax.nn.gelu(v_ref[rows, :]), lg_ref[...], lb_ref[...]).astype(BF16)
        parts = [_dot(wc_ref[g], v[:, g * LANES:(g + 1) * LANES]) for g in range(SGU_GROUPS)]
        mix = jnp.concatenate(parts, axis=1) + bs_ref[...]
        o_ref[rows, :] = (u * mix).astype(o_ref.dtype)


def sgu_mixer(z, w_causal, bias_rows, ln_g, ln_b, *, rows, tt=256):
    vec = pl.BlockSpec((1, W_MIX), lambda i: (0, 0))
    return pl.pallas_call(
        functools.partial(_sgu_kernel, tt=tt), grid=(rows // tt,),
        in_specs=[pl.BlockSpec((tt, W_MIX), lambda i: (i, C_U // W_MIX)),
                  pl.BlockSpec((tt, W_MIX), lambda i: (i, C_V // W_MIX)),
                  pl.BlockSpec((SGU_GROUPS, SGU_CHUNK, SGU_CHUNK), lambda i: (0, 0, 0)),
                  pl.BlockSpec((SGU_CHUNK, W_MIX), lambda i: (0, 0)), vec, vec],
        out_specs=pl.BlockSpec((tt, W_MIX), lambda i: (i, 0)),
        out_shape=jax.ShapeDtypeStruct((rows, W_MIX), BF16), compiler_params=_cp(1), name="sgu")(
            z, z, w_causal, bias_rows, ln_g.reshape(1, -1), ln_b.reshape(1, -1))


def _sgu_step_kernel(u_ref, v_ref, w0_ref, b0_ref, lg_ref, lb_ref, o_ref, vo_ref):
    u = jax.nn.gelu(u_ref[...])
    v = _layernorm(jax.nn.gelu(v_ref[...]), lg_ref[...], lb_ref[...])
    vo_ref[...] = v
    o_ref[...] = (u * (v * w0_ref[...] + b0_ref[...])).astype(o_ref.dtype)


def sgu_step(zs, w0, b0, ln_g, ln_b):
    n = zs.shape[0]
    vec = pl.BlockSpec((1, W_MIX), lambda i: (0, 0))
    return pl.pallas_call(
        _sgu_step_kernel, grid=(1,),
        in_specs=[pl.BlockSpec((n, W_MIX), lambda i: (0, C_U // W_MIX)),
                  pl.BlockSpec((n, W_MIX), lambda i: (0, C_V // W_MIX)), vec, vec, vec, vec],
        out_specs=[pl.BlockSpec((n, W_MIX), lambda i: (0, 0))] * 2,
        out_shape=[jax.ShapeDtypeStruct((n, W_MIX), BF16), jax.ShapeDtypeStruct((n, W_MIX), F32)],
        compiler_params=_cp(1), name="sgu_step")(zs, zs, w0, b0, ln_g.reshape(1, -1), ln_b.reshape(1, -1))


def _log_sigmoid(x):
    return jnp.minimum(x, 0.0) - jnp.log1p(jnp.exp(-jnp.abs(x)))


def _gla_kernel(q_ref, k_ref, v_ref, r_ref, da_ref, wa_ref, ba_ref, gn_ref, s0_ref, o_ref, so_ref, st, *, tt, chunk, n_valid):
    t = pl.program_id(1)

    @pl.when(t == 0)
    def _():
        st[...] = s0_ref[...]

    row = lax.broadcasted_iota(jnp.int32, (chunk, chunk), 0)
    col = lax.broadcasted_iota(jnp.int32, (chunk, chunk), 1)
    causal = row >= col
    tril = causal.astype(BF16)
    for c0 in range(0, tt, chunk):
        rows = slice(c0, c0 + chunk)
        la = _log_sigmoid(_dot(da_ref[rows, :].astype(BF16), wa_ref[...]) + ba_ref[...]) * (1.0 / GLA_TAU)
        if n_valid < chunk:
            la = jnp.where(lax.broadcasted_iota(jnp.int32, la.shape, 0) < n_valid, la, 0.0)
        b = _split_dot_left(tril, la)
        bl = b[chunk - 1:chunk, :]
        kk = k_ref[rows, :]
        q_in = (q_ref[rows, :] * (GLA_DK ** -0.5) * jnp.exp(b)).astype(BF16)
        k_in = (kk * jnp.exp(-b)).astype(BF16)
        k_end = (kk * jnp.exp(bl - b)).astype(BF16)
        decay = jnp.exp(bl)
        vv = v_ref[rows, :].astype(BF16)
        for h in range(GLA_HEADS):
            hs = slice(h * LANES, (h + 1) * LANES)
            a = jnp.where(causal, _dot_nt(q_in[:, hs], k_in[:, hs]), 0.0).astype(BF16)
            o = _dot(a, vv[:, hs]) + _dot_nt(q_in[:, hs], st[h].astype(BF16))
            st[h] = st[h] * decay[:, hs] + _dot_tn(vv[:, hs], k_end[:, hs])
            y = o * lax.rsqrt(jnp.mean(o * o, axis=-1, keepdims=True) + EPS) * gn_ref[...]
            o_ref[rows, hs] = (y * _silu(r_ref[rows, hs])).astype(o_ref.dtype)

    @pl.when(t == pl.num_programs(1) - 1)
    def _():
        so_ref[...] = st[...]


def _split_dot_left(w, x):
    hi = x.astype(BF16)
    lo = (x - hi.astype(F32)).astype(BF16)
    return _dot(w, hi) + _dot(w, lo)


def gla_mixer(z, s0_t, w_a2p, b_ap, g_norm, *, nb, t_len, tt, chunk, n_valid):
    nt = t_len // tt
    rowspec = lambda c: pl.BlockSpec((tt, W_MIX), lambda b, t: (b * nt + t, c // W_MIX))
    state = pl.BlockSpec((None, GLA_HEADS, GLA_DV, LANES), lambda b, t: (b, 0, 0, 0))
    return pl.pallas_call(
        functools.partial(_gla_kernel, tt=tt, chunk=chunk, n_valid=n_valid), grid=(nb, nt),
        in_specs=[rowspec(C_DQ), rowspec(C_DK), rowspec(C_DV), rowspec(C_DR),
                  pl.BlockSpec((tt, LANES), lambda b, t: (b * nt + t, C_DA // LANES)),
                  pl.BlockSpec((LANES, W_MIX), lambda b, t: (0, 0)),
                  pl.BlockSpec((1, W_MIX), lambda b, t: (0, 0)),
                  pl.BlockSpec((1, LANES), lambda b, t: (0, 0)), state],
        out_specs=[pl.BlockSpec((tt, W_MIX), lambda b, t: (b * nt + t, 0)), state],
        out_shape=[jax.ShapeDtypeStruct((nb * t_len, W_MIX), BF16),
                   jax.ShapeDtypeStruct((nb, GLA_HEADS, GLA_DV, LANES), F32)],
        scratch_shapes=[pltpu.VMEM((GLA_HEADS, GLA_DV, LANES), F32)],
        compiler_params=_cp(2), name="gla")(z, z, z, z, z, w_a2p, b_ap, g_norm.reshape(1, -1), s0_t)


SEG_PER_STEP = 128
CMP_ROWS = SEG_PER_STEP * CMP_STRIDE


def _compress_kernel(*refs, n_in, tail_from_next):
    pos = 1 if tail_from_next else 0
    in_refs = refs[pos:pos + n_in]
    w_ref, o_ref = refs[pos + n_in], refs[pos + n_in + 1]
    xs = refs[pos + n_in + 2:]
    n_main = n_in - 1 if tail_from_next else n_in
    rows_each = CMP_ROWS // n_main
    for ch in range(4):
        cols = slice(ch * LANES, (ch + 1) * LANES)
        for j in range(n_main):
            xs[ch][j * rows_each:(j + 1) * rows_each, :] = in_refs[j][:, cols]
        if tail_from_next:
            xs[ch][CMP_ROWS:CMP_ROWS + CMP_STRIDE, :] = in_refs[n_in - 1][0:CMP_STRIDE, cols]
        else:
            xs[ch][CMP_ROWS:CMP_ROWS + CMP_STRIDE, :] = jnp.zeros((CMP_STRIDE, LANES), F32)
        acc = jnp.zeros((SEG_PER_STEP, LANES), F32)
        for r in range(CMP_LEN):
            x = xs[ch][pl.ds(r, SEG_PER_STEP, stride=CMP_STRIDE), :]
            acc = acc + _dot(x.astype(BF16), w_ref[ch // 2, r])
        o_ref[:, cols] = acc


def compress_prompt(z, w_cmp):
    return pl.pallas_call(
        functools.partial(_compress_kernel, n_in=1, tail_from_next=False), grid=(BATCH,),
        in_specs=[pl.BlockSpec((CMP_ROWS, 4 * LANES), lambda b: (b, C_CMP // (4 * LANES))),
                  pl.BlockSpec((2, CMP_LEN, LANES, LANES), lambda b: (0, 0, 0, 0))],
        out_specs=pl.BlockSpec((None, SEG_PER_STEP, 4 * LANES), lambda b: (b, 0, 0)),
        out_shape=jax.ShapeDtypeStruct((BATCH, SEG_PER_STEP, 4 * LANES), F32),
        scratch_shapes=[pltpu.VMEM((CMP_ROWS + CMP_STRIDE, LANES), F32)] * 4,
        compiler_params=_cp(1), name="compress_prompt")(z, w_cmp)


PAGES_PER_STEP = CMP_ROWS // PAGE_SIZE
N_PAGES = PAST_LEN // PAGE_SIZE


def compress_paged(cache, page_table, w_cmp):
    n_steps = N_PAGES // PAGES_PER_STEP

    def page_spec(j):
        return pl.BlockSpec((None, PAGE_SIZE, 4 * LANES),
                            lambda b, c, pt: (pt[b, jnp.minimum(c * PAGES_PER_STEP + j, N_PAGES - 1)], 0, 0))

    grid_spec = pltpu.PrefetchScalarGridSpec(
        num_scalar_prefetch=1, grid=(DEC_BATCH, n_steps),
        in_specs=[page_spec(j) for j in range(PAGES_PER_STEP + 1)]
        + [pl.BlockSpec((2, CMP_LEN, LANES, LANES), lambda b, c, pt: (0, 0, 0, 0))],
        out_specs=pl.BlockSpec((None, SEG_PER_STEP, 4 * LANES), lambda b, c, pt: (b, c, 0)),
        scratch_shapes=[pltpu.VMEM((CMP_ROWS + CMP_STRIDE, LANES), F32)] * 4)
    return pl.pallas_call(
        functools.partial(_compress_kernel, n_in=PAGES_PER_STEP + 1, tail_from_next=True), grid_spec=grid_spec,
        out_shape=jax.ShapeDtypeStruct((DEC_BATCH, n_steps * SEG_PER_STEP, 4 * LANES), F32),
        compiler_params=_cp(2), name="compress_paged")(page_table, *([cache] * (PAGES_PER_STEP + 1)), w_cmp)


def _top_select(score, n_lanes):
    lane = lax.broadcasted_iota(jnp.int32, score.shape, 1).astype(F32)
    sel = jnp.zeros(score.shape, F32)
    for _ in range(N_SELECT):
        m = jnp.max(score, axis=-1, keepdims=True)
        idx = jnp.min(jnp.where(score == m, lane, float(n_lanes)), axis=-1, keepdims=True)
        pick = lane == idx
        sel = jnp.where(pick, 1.0, sel)
        score = jnp.where(pick, -3e38, score)
    return sel


def _bias_tiles(delta_of_tile, n_tiles, t0, t1, t2):
    tiles = []
    for j in range(n_tiles):
        delta = delta_of_tile(j)
        tiles.append(jnp.where(delta == 0, t0, jnp.where(delta == 1, t1, t2)))
    return jnp.concatenate(tiles, axis=1)


WIN_BAND = WINDOW + Q_BLOCK


def _nsa_prompt_kernel(q_ref, cbk_ref, cbv_ref, ks_ref, vs_ref, kw_ref, vw_ref, g_ref, bc_ref, t0_ref, t1_ref, t2_ref,
                       cov_ref, e_ref, o_ref):
    h = pl.program_id(1)
    i = pl.program_id(2)
    q = q_ref[...] * (HEAD_DIM ** -0.5)
    qs = jnp.concatenate([q[:, g * LANES:(g + 1) * LANES] for g in range(GRP)], axis=0).astype(BF16)
    qpos = i * Q_BLOCK + lax.broadcasted_iota(jnp.int32, (Q_BLOCK, 1), 0)
    lane = lax.broadcasted_iota(jnp.int32, (Q_BLOCK, LANES), 1)

    sc = _dot_nt(qs, cbk_ref[...].astype(BF16))
    cmask = (lane * CMP_STRIDE + (CMP_LEN - 1) <= qpos) & (lane < SEQ // CMP_STRIDE - 1)
    p_c = [_masked_softmax(sc[g * Q_BLOCK:(g + 1) * Q_BLOCK] + bc_ref[g], cmask) for g in range(GRP)]
    o_cmp = _dot(jnp.concatenate(p_c, axis=0).astype(BF16), cbv_ref[...].astype(BF16))
    imp = _split_dot(p_c[0] + p_c[1] + p_c[2] + p_c[3], cov_ref[...])

    cur = lax.shift_right_logical(qpos, 6)
    valid = lane * SLC_LEN <= qpos
    forced = (lane == 0) | (lane == cur) | (lane == cur - 1)
    score = jnp.where(valid, imp + jnp.where(forced, FORCE_BONUS, 0.0), NEG)
    sel = _top_select(score, LANES) * valid.astype(F32)

    kpos = lax.broadcasted_iota(jnp.int32, (Q_BLOCK, SEQ), 1)
    smask = (_dot(sel.astype(BF16), e_ref[...]) > 0.5) & (kpos <= qpos)
    ss = _dot_nt(qs, ks_ref[...].astype(BF16))
    p_s = []
    for g in range(GRP):
        bias = _bias_tiles(lambda j: i - j, SEQ // Q_BLOCK, t0_ref[g], t1_ref[g], t2_ref[g])
        p_s.append(_masked_softmax(ss[g * Q_BLOCK:(g + 1) * Q_BLOCK] + bias, smask))
    o_slc = _dot(jnp.concatenate(p_s, axis=0).astype(BF16), vs_ref[...].astype(BF16))

    first_tile = jnp.maximum(i - WINDOW // Q_BLOCK, 0)
    start = pl.multiple_of(first_tile * Q_BLOCK, Q_BLOCK)
    dist = qpos - (start + lax.broadcasted_iota(jnp.int32, (Q_BLOCK, WIN_BAND), 1))
    wmask = (dist >= 0) & (dist <= WINDOW)
    sw = _dot_nt(qs, kw_ref[pl.ds(start, WIN_BAND), :].astype(BF16))
    p_w = []
    for g in range(GRP):
        bias = _bias_tiles(lambda j: i - first_tile - j, WIN_BAND // Q_BLOCK, t0_ref[g], t1_ref[g], t2_ref[g])
        p_w.append(_masked_softmax(sw[g * Q_BLOCK:(g + 1) * Q_BLOCK] + bias, wmask))
    o_win = _dot(jnp.concatenate(p_w, axis=0).astype(BF16), vw_ref[pl.ds(start, WIN_BAND), :].astype(BF16))

    gates = jax.nn.sigmoid(g_ref[...])
    for g in range(GRP):
        rows = slice(g * Q_BLOCK, (g + 1) * Q_BLOCK)
        base = (h * GRP + g) * 3
        gc = [jnp.sum(jnp.where(lane == base + br, gates, 0.0), axis=-1, keepdims=True) for br in range(3)]
        o = gc[0] * o_cmp[rows] + gc[1] * o_slc[rows] + gc[2] * o_win[rows]
        o_ref[:, g * LANES:(g + 1) * LANES] = o.astype(o_ref.dtype)


def nsa_prompt(z, cb, bias_c, t0, t1, t2, cover, expand):
    nq = SEQ // Q_BLOCK
    kv = lambda c: pl.BlockSpec((SEQ, LANES), lambda b, h, i, c=c: (b, c // LANES + h))
    cbs = lambda c: pl.BlockSpec((None, SEG_PER_STEP, LANES), lambda b, h, i, c=c: (b, 0, c + h))
    toe = pl.BlockSpec((GRP, Q_BLOCK, LANES), lambda b, h, i: (h, 0, 0))
    return pl.pallas_call(
        _nsa_prompt_kernel, grid=(BATCH, KVH, nq),
        in_specs=[pl.BlockSpec((Q_BLOCK, GRP * LANES), lambda b, h, i: (b * nq + i, C_Q // (GRP * LANES) + h)),
                  cbs(0), cbs(KVH), kv(C_SLC), kv(C_SLC + KVH * LANES), kv(C_WIN), kv(C_WIN + KVH * LANES),
                  pl.BlockSpec((Q_BLOCK, LANES), lambda b, h, i: (b * nq + i, C_CG // LANES)),
                  pl.BlockSpec((GRP, Q_BLOCK, LANES), lambda b, h, i: (h, i, 0)), toe, toe, toe,
                  pl.BlockSpec((LANES, LANES), lambda b, h, i: (0, 0)),
                  pl.BlockSpec((LANES, SEQ), lambda b, h, i: (0, 0))],
        out_specs=pl.BlockSpec((Q_BLOCK, GRP * LANES), lambda b, h, i: (b * nq + i, h)),
        out_shape=jax.ShapeDtypeStruct((M_PROMPT, W_MIX), BF16), compiler_params=_cp(3), name="nsa_prompt")(
            z, cb, cb, z, z, z, z, z, bias_c, t0, t1, t2, cover, expand)


N_CMP_S = PAST_LEN // CMP_STRIDE
N_SLC_S = 384


def _head_rows(shape):
    return lax.broadcasted_iota(jnp.int32, shape, 0) < GRP


def _nsa_step_a_kernel(q_ref, cb_ref, bc_ref, cov_ref, wb_ref, kvn_ref, bw_ref, bwn_ref, ocw_ref, sel_ref):
    q = q_ref[...] * (HEAD_DIM ** -0.5)
    qb = q.astype(BF16)
    cb = cb_ref[...]
    first = _head_rows((NSA_HEADS, N_CMP_S))
    s = jnp.where(first, _dot_nt(qb, cb[:, 0:LANES].astype(BF16)), _dot_nt(qb, cb[:, LANES:2 * LANES].astype(BF16)))
    kidx = lax.broadcasted_iota(jnp.int32, (NSA_HEADS, N_CMP_S), 1)
    p = _masked_softmax(s + bc_ref[...], kidx * CMP_STRIDE + (CMP_LEN - 1) <= PAST_LEN)
    pb = p.astype(BF16)
    o_cmp = jnp.where(_head_rows((NSA_HEADS, LANES)), _dot(pb, cb[:, 2 * LANES:3 * LANES].astype(BF16)),
                      _dot(pb, cb[:, 3 * LANES:4 * LANES].astype(BF16)))
    row = lax.broadcasted_iota(jnp.int32, (NSA_HEADS, N_CMP_S), 0)
    ps0 = jnp.sum(jnp.where(first, p, 0.0), axis=0, keepdims=True)
    ps1 = jnp.sum(jnp.where(first, 0.0, p), axis=0, keepdims=True)
    psum = jnp.where(row == 0, ps0, jnp.where(row == 1, ps1, 0.0))
    imp = _split_dot(psum, cov_ref[...])
    lane = lax.broadcasted_iota(jnp.int32, (NSA_HEADS, N_SLC_S), 1)
    cur = PAST_LEN // SLC_LEN
    valid = lane <= cur
    forced = (lane == 0) | (lane == cur) | (lane == cur - 1)
    score = jnp.where(valid, imp + jnp.where(forced, FORCE_BONUS, 0.0), NEG)
    sel_ref[...] = _top_select(score, N_SLC_S) * valid.astype(F32)

    wb = wb_ref[...]
    firstw = _head_rows((NSA_HEADS, WINDOW))
    sw = jnp.where(firstw, _dot_nt(qb, wb[:, 0:LANES].astype(BF16)), _dot_nt(qb, wb[:, LANES:2 * LANES].astype(BF16)))
    sw = sw + bw_ref[...]
    firstl = _head_rows((NSA_HEADS, LANES))
    kvn = kvn_ref[...]
    k_new = jnp.where(firstl, kvn[:, 0:LANES], kvn[:, LANES:2 * LANES])
    v_new = jnp.where(firstl, kvn[:, 2 * LANES:3 * LANES], kvn[:, 3 * LANES:4 * LANES])
    s_new = jnp.sum(q * k_new, axis=-1, keepdims=True) + bwn_ref[:, 0:1]
    m = jnp.maximum(jnp.max(sw, axis=-1, keepdims=True), s_new)
    pw = jnp.exp(sw - m)
    pn = jnp.exp(s_new - m)
    pwb = pw.astype(BF16)
    acc = jnp.where(firstl, _dot(pwb, wb[:, 2 * LANES:3 * LANES].astype(BF16)), _dot(pwb, wb[:, 3 * LANES:4 * LANES].astype(BF16)))
    o_win = (acc + pn * v_new) / (jnp.sum(pw, axis=-1, keepdims=True) + pn)
    ocw_ref[:, 0:LANES] = o_cmp
    ocw_ref[:, LANES:2 * LANES] = o_win


def nsa_step_a(q_s, cb_s, bias_c, cover, win_buf, kvw_new, bias_w, bias_wn):
    per_b = lambda *shape: pl.BlockSpec((None,) + shape, lambda b: (b,) + (0,) * len(shape))
    whole = lambda *shape: pl.BlockSpec(shape, lambda b: (0,) * len(shape))
    return pl.pallas_call(
        _nsa_step_a_kernel, grid=(DEC_BATCH,),
        in_specs=[per_b(NSA_HEADS, LANES), per_b(N_CMP_S, 4 * LANES), whole(NSA_HEADS, N_CMP_S), whole(N_CMP_S, N_SLC_S),
                  per_b(WINDOW, 4 * LANES), per_b(1, 4 * LANES), whole(NSA_HEADS, WINDOW), whole(NSA_HEADS, LANES)],
        out_specs=[per_b(NSA_HEADS, 2 * LANES), per_b(NSA_HEADS, N_SLC_S)],
        out_shape=[jax.ShapeDtypeStruct((DEC_BATCH, NSA_HEADS, 2 * LANES), F32),
                   jax.ShapeDtypeStruct((DEC_BATCH, NSA_HEADS, N_SLC_S), F32)],
        compiler_params=_cp(1), name="nsa_step_a")(q_s, cb_s, bias_c, cover, win_buf, kvw_new, bias_w, bias_wn)


def _nsa_step_b_kernel(*refs):
    pages = refs[1:1 + PAGES_PER_STEP]
    (q_ref, sel_ref, e_ref, bs_ref, kvn_ref, seln_ref, bn_ref, ocw_ref, g_ref, o_ref, m_sc, l_sc, acc_sc) = refs[1 + PAGES_PER_STEP:]
    c = pl.program_id(1)

    @pl.when(c == 0)
    def _():
        m_sc[...] = jnp.full(m_sc.shape, NEG, F32)
        l_sc[...] = jnp.zeros(l_sc.shape, F32)
        acc_sc[...] = jnp.zeros(acc_sc.shape, F32)

    q = q_ref[...] * (HEAD_DIM ** -0.5)
    qb = q.astype(BF16)
    kv = [jnp.concatenate([pg[:, j * LANES:(j + 1) * LANES].astype(BF16) for pg in pages], axis=0) for j in range(4)]
    first = _head_rows((NSA_HEADS, CMP_ROWS))
    firstl = _head_rows((NSA_HEADS, LANES))
    s = jnp.where(first, _dot_nt(qb, kv[0]), _dot_nt(qb, kv[1])) + bs_ref[...]
    mask = _dot(sel_ref[...].astype(BF16), e_ref[...]) > 0.5
    s = jnp.where(mask, s, NEG)
    m_old = m_sc[...]
    m_new = jnp.maximum(m_old, jnp.max(s, axis=-1, keepdims=True))
    alpha = jnp.exp(m_old - m_new)
    p = jnp.where(mask, jnp.exp(s - m_new), 0.0)
    pb = p.astype(BF16)
    l_sc[...] = alpha * l_sc[...] + jnp.sum(p, axis=-1, keepdims=True)
    acc_sc[...] = alpha * acc_sc[...] + jnp.where(firstl, _dot(pb, kv[2]), _dot(pb, kv[3]))
    m_sc[...] = m_new

    @pl.when(c == pl.num_programs(1) - 1)
    def _():
        kvn = kvn_ref[...]
        k_new = jnp.where(firstl, kvn[:, 0:LANES], kvn[:, LANES:2 * LANES])
        v_new = jnp.where(firstl, kvn[:, 2 * LANES:3 * LANES], kvn[:, 3 * LANES:4 * LANES])
        on = seln_ref[:, 0:1] > 0.5
        s_n = jnp.where(on, jnp.sum(q * k_new, axis=-1, keepdims=True) + bn_ref[:, 0:1], NEG)
        m_old = m_sc[...]
        m_fin = jnp.maximum(m_old, s_n)
        alpha = jnp.exp(m_old - m_fin)
        p_n = jnp.where(on, jnp.exp(s_n - m_fin), 0.0)
        l_fin = alpha * l_sc[...] + p_n
        o_slc = (alpha * acc_sc[...] + p_n * v_new) / jnp.maximum(l_fin, 1e-30)
        gates = jax.nn.sigmoid(g_ref[...])
        o = gates[:, 0:1] * ocw_ref[:, 0:LANES] + gates[:, 1:2] * o_slc + gates[:, 2:3] * ocw_ref[:, LANES:2 * LANES]
        o_ref[...] = o.astype(o_ref.dtype)


def nsa_step_b(cache, page_table, q_s, sel_r, expand, bias_s, kvs_new, sel_new, bias_n, ocw, gates):
    n_steps = N_PAGES // PAGES_PER_STEP

    def page_spec(j):
        return pl.BlockSpec((None, PAGE_SIZE, 4 * LANES), lambda b, c, pt: (pt[b, c * PAGES_PER_STEP + j], 0, 0))

    per_b = lambda *shape: pl.BlockSpec((None,) + shape, lambda b, c, pt: (b,) + (0,) * len(shape))
    whole = lambda *shape: pl.BlockSpec(shape, lambda b, c, pt: (0,) * len(shape))
    grid_spec = pltpu.PrefetchScalarGridSpec(
        num_scalar_prefetch=1, grid=(DEC_BATCH, n_steps),
        in_specs=[page_spec(j) for j in range(PAGES_PER_STEP)] + [
            per_b(NSA_HEADS, LANES),
            pl.BlockSpec((None, None, NSA_HEADS, LANES), lambda b, c, pt: (b, c, 0, 0)),
            whole(LANES, CMP_ROWS),
            pl.BlockSpec((NSA_HEADS, CMP_ROWS), lambda b, c, pt: (0, c)),
            per_b(1, 4 * LANES), per_b(NSA_HEADS, LANES), whole(NSA_HEADS, LANES),
            per_b(NSA_HEADS, 2 * LANES), per_b(NSA_HEADS, LANES)],
        out_specs=per_b(NSA_HEADS, LANES),
        scratch_shapes=[pltpu.VMEM((NSA_HEADS, 1), F32), pltpu.VMEM((NSA_HEADS, 1), F32), pltpu.VMEM((NSA_HEADS, LANES), F32)])
    return pl.pallas_call(
        _nsa_step_b_kernel, grid_spec=grid_spec,
        out_shape=jax.ShapeDtypeStruct((DEC_BATCH, NSA_HEADS, LANES), BF16),
        compiler_params=_cp(2), name="nsa_step_b")(
            page_table, *([cache] * PAGES_PER_STEP), q_s, sel_r, expand, bias_s, kvs_new, sel_new, bias_n, ocw, gates)


def _rel_table(rel_bias, n):
    d = jnp.arange(n)
    max_exact = REL_BUCKETS // 2
    nf = jnp.maximum(d, 1).astype(F32)
    large = max_exact + (jnp.log(nf / max_exact) / math.log(REL_MAX_DIST / max_exact)
                         * (REL_BUCKETS - max_exact)).astype(jnp.int32)
    bucket = jnp.where(d < max_exact, d, jnp.minimum(large, REL_BUCKETS - 1))
    return rel_bias[bucket]


def _nsa_tables(rel_bias):
    tab = _rel_table(rel_bias, PAST_LEN + 1)
    r = jnp.arange(Q_BLOCK)
    diff = r[:, None] - r[None, :]
    heads_first = lambda a: jnp.moveaxis(a, -1, 0)
    t0 = heads_first(tab[jnp.maximum(diff, 0)])
    t1 = heads_first(tab[Q_BLOCK + diff])
    t2 = jnp.broadcast_to(tab[2 * Q_BLOCK][:, None, None], (NSA_HEADS, Q_BLOCK, LANES))
    dist_c = jnp.arange(SEQ)[:, None] - (jnp.arange(LANES)[None, :] * CMP_STRIDE + CMP_LEN - 1)
    bias_c = heads_first(tab[jnp.maximum(dist_c, 0)])
    k = jnp.arange(LANES)
    j = jnp.arange(LANES)
    cover = ((k[:, None] * CMP_STRIDE < (j[None, :] + 1) * SLC_LEN) & (k[:, None] * CMP_STRIDE + CMP_LEN > j[None, :] * SLC_LEN)
             & (k[:, None] < SEQ // CMP_STRIDE - 1) & (j[None, :] < SEQ // SLC_LEN)).astype(BF16)
    expand = (jnp.arange(SEQ)[None, :] // SLC_LEN == jnp.arange(LANES)[:, None]).astype(BF16)
    ks = jnp.arange(N_CMP_S)
    bias_cs = heads_first(tab[jnp.maximum(PAST_LEN - (ks * CMP_STRIDE + CMP_LEN - 1), 0)])
    js = jnp.arange(N_SLC_S)
    cover_s = ((ks[:, None] * CMP_STRIDE < (js[None, :] + 1) * SLC_LEN) & (ks[:, None] * CMP_STRIDE + CMP_LEN > js[None, :] * SLC_LEN)
               & (ks[:, None] < N_CMP_S - 1) & (js[None, :] <= PAST_LEN // SLC_LEN)).astype(BF16)
    bias_w = heads_first(tab[WINDOW - jnp.arange(WINDOW)])
    bias_0 = jnp.broadcast_to(tab[0][:, None], (NSA_HEADS, LANES))
    bias_s = heads_first(tab[PAST_LEN - jnp.arange(PAST_LEN)])
    return dict(t0=t0, t1=t1, t2=t2, bias_c=bias_c, cover=cover, expand=expand, bias_cs=bias_cs, cover_s=cover_s,
                bias_w=bias_w, bias_0=bias_0, bias_s=bias_s)


def _pad_heads(w):
    lead = w.shape[:-1]
    w = w.reshape(lead + (GLA_HEADS, GLA_DK))
    w = jnp.pad(w, [(0, 0)] * len(lead) + [(0, 0), (0, LANES - GLA_DK)])
    return w.reshape(lead + (GLA_HEADS * LANES,))


def _pad_cols(w, n):
    return jnp.pad(w, ((0, 0), (0, n - w.shape[1])))


def _repack_w_in(w):
    seg = lambda a, n: w[:, a:a + n]
    parts = [seg(0, 2048), seg(2048, 2048), seg(4096, 1024), seg(7704, 1024), seg(8744, 1024),
             _pad_heads(seg(6680, 512)), _pad_heads(seg(7192, 512)), seg(5120, 512), seg(5632, 512), seg(6144, 512),
             _pad_cols(seg(6656, 24), LANES), _pad_cols(seg(8728, 16), LANES)]
    w1 = jnp.concatenate(parts, axis=1)
    return _pad_cols(w1, ZW).astype(BF16), w[:, G_OFF:].astype(BF16)


def _mixers(i, z, tabs, wts, st):
    new = {}
    zs = z[M_PROMPT:M_PROMPT + DEC_BATCH]
    zs_pad = jnp.zeros((DEC_BATCH, 8, ZW), F32).at[:, 0].set(zs).reshape(DEC_BATCH * 8, ZW)
    tail = jnp.zeros((M_SLAB - M_PROMPT - DEC_BATCH, W_MIX), BF16)
    slab = lambda p, s: jnp.concatenate([p, s, tail], axis=0)

    zero_halo = jnp.zeros((BATCH, CONV_HALO, W_MIX), F32)
    conv_args = (wts['conv_w'], wts['conv_b'], wts['conv_ln_g'], wts['conv_ln_b'])
    oa_p, cst_p = conv_mixer(z, zero_halo, *conv_args, nb=BATCH, t_len=SEQ, tt=256, rb=32, n_last=256)
    halo_s = jnp.pad(st['conv'], ((0, 0), (CONV_HALO - (CONV_WIDTH - 1), 0), (0, 0)))
    oa_s, cst_s = conv_mixer(zs_pad, halo_s, *conv_args, nb=DEC_BATCH, t_len=8, tt=8, rb=8, n_last=1)
    new['conv_p'] = cst_p[:, CONV_HALO - (CONV_WIDTH - 1):]
    new['conv_s'] = cst_s[:, CONV_HALO - (CONV_WIDTH - 1):]
    o_a = slab(oa_p, oa_s[::8])

    ws = wts['sgu_ws']
    w_causal = (ws * jnp.tril(jnp.ones((SGU_CHUNK, SGU_CHUNK), F32))).astype(BF16)
    bias_rows = jnp.repeat(wts['sgu_bs'].T, LANES, axis=1)
    ob_p = sgu_mixer(z, w_causal, bias_rows, wts['sgu_ln_g'], wts['sgu_ln_b'], rows=M_PROMPT)
    w0 = jnp.repeat(ws[:, 0, 0], LANES)[None, :]
    ob_s, v_s = sgu_step(zs, w0, bias_rows[0:1], wts['sgu_ln_g'], wts['sgu_ln_b'])
    new['sgu_v'] = v_s[:, None, :]
    o_b = slab(ob_p, ob_s)

    w_a2p = jnp.zeros((LANES, GLA_HEADS * LANES), F32).at[:GLA_RANK].set(_pad_heads(wts['gla_w_a2'])).astype(BF16)
    b_ap = _pad_heads(wts['gla_b_a'])[None, :]
    s0_p = jnp.zeros((BATCH, GLA_HEADS, GLA_DV, LANES), F32)
    od_p, sp = gla_mixer(z, s0_p, w_a2p, b_ap, wts['gla_norm'], nb=BATCH, t_len=SEQ, tt=256, chunk=GLA_CHUNK,
                         n_valid=GLA_CHUNK)
    s0_s = jnp.pad(jnp.swapaxes(st['gla'], -1, -2), ((0, 0), (0, 0), (0, 0), (0, LANES - GLA_DK)))
    od_s, ss = gla_mixer(zs_pad, s0_s, w_a2p, b_ap, wts['gla_norm'], nb=DEC_BATCH, t_len=8, tt=8, chunk=8, n_valid=1)
    new['gla_p'] = jnp.swapaxes(sp[..., :GLA_DK], -1, -2)
    new['gla_s'] = jnp.swapaxes(ss[..., :GLA_DK], -1, -2)
    o_d = slab(od_p, od_s[::8])

    w_cmp = wts['nsa_w_cmp'].astype(BF16)
    cb_p = compress_prompt(z, w_cmp)
    oc_p = nsa_prompt(z, cb_p, tabs['bias_c'], tabs['t0'], tabs['t1'], tabs['t2'], tabs['cover'], tabs['expand'])
    pt = st['page_table']
    cache_cmp = st['cmp'].reshape(-1, PAGE_SIZE, 4 * LANES)
    cache_slc = st['slc'].reshape(-1, PAGE_SIZE, 4 * LANES)
    cb_s = compress_paged(cache_cmp, pt, w_cmp)
    q_s = zs[:, C_Q:C_Q + NSA_HEADS * LANES].reshape(DEC_BATCH, NSA_HEADS, LANES)
    kvw_new = zs[:, None, C_WIN:C_WIN + 4 * LANES]
    kvs_new = zs[:, None, C_SLC:C_SLC + 4 * LANES]
    win_buf = st['win'].reshape(DEC_BATCH, WINDOW, 4 * LANES)
    ocw, sel = nsa_step_a(q_s, cb_s, tabs['bias_cs'], tabs['cover_s'], win_buf, kvw_new, tabs['bias_w'], tabs['bias_0'])
    n_steps = N_PAGES // PAGES_PER_STEP
    blocks_per_step = CMP_ROWS // SLC_LEN
    sel_kv = jnp.repeat(sel[:, :KVH], GRP, axis=1)
    sel_r = sel_kv[:, :, :n_steps * blocks_per_step].reshape(DEC_BATCH, NSA_HEADS, n_steps, blocks_per_step)
    sel_r = jnp.pad(jnp.swapaxes(sel_r, 1, 2), ((0, 0), (0, 0), (0, 0), (0, LANES - blocks_per_step)))
    sel_new = jnp.broadcast_to(sel_kv[:, :, PAST_LEN // SLC_LEN, None], (DEC_BATCH, NSA_HEADS, LANES))
    gates_s = _pad_cols(zs[:, C_CG:C_CG + 3 * NSA_HEADS].reshape(DEC_BATCH * NSA_HEADS, 3), LANES).reshape(DEC_BATCH, NSA_HEADS, LANES)
    oc_s = nsa_step_b(cache_slc, pt, q_s, sel_r, tabs['expand'], tabs['bias_s'], kvs_new, sel_new, tabs['bias_0'], ocw, gates_s)
    o_c = slab(oc_p, oc_s.reshape(DEC_BATCH, W_MIX))

    kvp = lambda c: z[:M_PROMPT, c:c + 4 * LANES]
    new['cmp_p'] = kvp(C_CMP).reshape(BATCH, SEQ // PAGE_SIZE, PAGE_SIZE, 2, KVH, HEAD_DIM)
    new['slc_p'] = kvp(C_SLC).reshape(BATCH, SEQ // PAGE_SIZE, PAGE_SIZE, 2, KVH, HEAD_DIM)
    new['win_p'] = kvp(C_WIN).reshape(BATCH, SEQ, 2, KVH, HEAD_DIM)[:, SEQ - WINDOW:]
    kvs = lambda c: zs[:, c:c + 4 * LANES].reshape(DEC_BATCH, 1, 2, KVH, HEAD_DIM)
    new['cmp_s'] = kvs(C_CMP)
    new['slc_s'] = kvs(C_SLC)
    new['win_s'] = jnp.concatenate([st['win'][:, 1:], kvs(C_WIN)], axis=1)
    return (o_a, o_b, o_c, o_d), new


def kernel(x_prompt, x_sample, cache_cmp_kv, cache_slc_kv, state_win_kv, state_conv, state_gla, page_table, p_prompt, p_sample, ffn1_norm, ffn1_w_in, ffn1_w_out, mix_norm, w_in, conv_w, conv_b, conv_ln_g, conv_ln_b, conv_w_out, sgu_ln_g, sgu_ln_b, sgu_ws, sgu_bs, sgu_w_out, nsa_w_cmp, nsa_w_out, gla_w_a2, gla_b_a, gla_norm, gla_w_out, w_out, ffn2_norm, ffn2_w_in, ffn2_w_out, pe_norm, w_pe, w_pe_gate, rel_bias, final_norm):
    n_tail = M_SLAB - M_PROMPT - DEC_BATCH
    x = jnp.concatenate([x_prompt.reshape(M_PROMPT, D_MODEL), x_sample.reshape(DEC_BATCH, D_MODEL),
                         jnp.zeros((n_tail, D_MODEL), F32)], axis=0)
    tabs = _nsa_tables(rel_bias)
    delta = None
    news = []
    for i in range(DEPTH):
        bf = lambda a: a[i].astype(BF16)
        x, h = add_norm(x, delta, ffn1_norm[i])
        delta = ffn_half(h, bf(ffn1_w_in), bf(ffn1_w_out))
        x, h = add_norm(x, delta, mix_norm[i])
        w1, wg = _repack_w_in(w_in[i])
        z = matmul(h, w1, tn=1024)
        gates = matmul(h, wg, tn=1024, act="sigmoid", out_dtype=BF16)
        wts = dict(conv_w=conv_w[i], conv_b=conv_b[i], conv_ln_g=conv_ln_g[i], conv_ln_b=conv_ln_b[i],
                   sgu_ws=sgu_ws[i], sgu_bs=sgu_bs[i], sgu_ln_g=sgu_ln_g[i], sgu_ln_b=sgu_ln_b[i],
                   nsa_w_cmp=nsa_w_cmp[i], gla_w_a2=gla_w_a2[i], gla_b_a=gla_b_a[i], gla_norm=gla_norm[i])
        st = dict(cmp=cache_cmp_kv[i], slc=cache_slc_kv[i], win=state_win_kv[i], conv=state_conv[i], gla=state_gla[i],
                  page_table=page_table)
        branches, new = _mixers(i, z, tabs, wts, st)
        news.append(new)
        merged = merge(branches, (bf(conv_w_out), bf(sgu_w_out), bf(nsa_w_out), bf(gla_w_out)), gates)
        delta = matmul(merged, bf(w_out), tn=1024)
        x, h = add_norm(x, delta, ffn2_norm[i])
        delta = ffn_half(h, bf(ffn2_w_in), bf(ffn2_w_out))
        x, h = add_norm(x, delta, pe_norm[i])
        ple = jnp.concatenate([p_prompt[i].reshape(M_PROMPT, PLE_DIM), p_sample[i].reshape(DEC_BATCH, PLE_DIM),
                               jnp.zeros((n_tail, PLE_DIM), F32)], axis=0).astype(BF16)
        delta = ple_delta(h, ple, bf(w_pe_gate), bf(w_pe))
    _, y = add_norm(x, delta, final_norm, write_x=False, out_dtype=F32)
    stack = lambda name: jnp.stack([n[name] for n in news], axis=0)
    return (y[:M_PROMPT].reshape(BATCH, SEQ, D_MODEL), y[M_PROMPT:M_PROMPT + DEC_BATCH].reshape(DEC_BATCH, 1, D_MODEL),
            stack('cmp_p'), stack('cmp_s'), stack('slc_p'), stack('slc_s'), stack('win_p'), stack('win_s'),
            stack('conv_p'), stack('conv_s'), stack('gla_p'), stack('gla_s'), stack('sgu_v'))
```

```python
import functools
import math

import jax
import jax.numpy as jnp
from jax import lax
from jax.experimental import pallas as pl
from jax.experimental.pallas import tpu as pltpu

F32 = jnp.float32
BF16 = jnp.bfloat16

D_MODEL = 4096
BATCH = 4
SEQ = 2048
DEPTH = 2
DEC_BATCH = 8
PAST_LEN = 16384
PAGE_SIZE = 128
D_FF = 11008
W_MIX = 1024
CONV_WIDTH = 31
SGU_CHUNK = 128
SGU_GROUPS = 8
HEAD_DIM = 128
NSA_HEADS = 8
KVH = 2
GRP = 4
CMP_STRIDE = 16
CMP_LEN = 32
SLC_LEN = 64
N_SELECT = 16
WINDOW = 512
Q_BLOCK = 128
FORCE_BONUS = 1.0e4
GLA_HEADS = 8
GLA_DV = 128
GLA_DK = 64
GLA_RANK = 16
GLA_TAU = 16.0
GLA_CHUNK = 64
REL_BUCKETS = 32
REL_MAX_DIST = 128
PLE_DIM = 256
EPS = 1e-6

LANES = 128
VMEM_LIMIT_BYTES = 56 * 1024 * 1024
M_PROMPT = BATCH * SEQ
M_SLAB = M_PROMPT + 16
TM = 912
TM_NORM = 304
FFN_CHUNK = 256
FFN_OUT_SLAB = 512
NEG = -1e30

C_VAL, C_GATE, C_U, C_V, C_Q, C_DV, C_DR, C_DQ, C_DK = 0, 1024, 2048, 3072, 4096, 5120, 6144, 7168, 8192
C_CMP, C_SLC, C_WIN, C_CG, C_DA = 9216, 9728, 10240, 10752, 10880
ZW = 11264
G_OFF = 9768

NT_DIMS = (((1,), (1,)), ((), ()))
TN_DIMS = (((0,), (0,)), ((), ()))


def _cp(n_axes, vmem=VMEM_LIMIT_BYTES):
    return pltpu.CompilerParams(dimension_semantics=("arbitrary",) * n_axes, vmem_limit_bytes=vmem)


def _dot(a, b):
    return jnp.dot(a, b, preferred_element_type=F32)


def _dot_nt(a, b):
    return lax.dot_general(a, b, NT_DIMS, preferred_element_type=F32)


def _dot_tn(a, b):
    return lax.dot_general(a, b, TN_DIMS, preferred_element_type=F32)


def _split_dot(x, w):
    hi = x.astype(BF16)
    lo = (x - hi.astype(F32)).astype(BF16)
    return _dot(hi, w) + _dot(lo, w)


def _silu(x):
    return x * jax.nn.sigmoid(x)


def _layernorm(x, g, b):
    mu = jnp.mean(x, axis=-1, keepdims=True)
    xc = x - mu
    var = jnp.mean(xc * xc, axis=-1, keepdims=True)
    return xc * lax.rsqrt(var + EPS) * g + b


def _masked_softmax(s, mask):
    s = jnp.where(mask, s, NEG)
    m = jnp.max(s, axis=-1, keepdims=True)
    p = jnp.where(mask, jnp.exp(s - m), 0.0)
    return p / jnp.maximum(jnp.sum(p, axis=-1, keepdims=True), 1e-30)


def _norm_kernel(*refs, has_delta, write_x):
    it = iter(refs)
    x_ref = next(it)
    d_ref = next(it) if has_delta else None
    g_ref = next(it)
    xo_ref = next(it) if write_x else None
    h_ref = next(it)
    x = x_ref[...]
    if has_delta:
        x = x + d_ref[...]
    if write_x:
        xo_ref[...] = x
    y = x * lax.rsqrt(jnp.mean(x * x, axis=-1, keepdims=True) + EPS) * g_ref[...]
    h_ref[...] = y.astype(h_ref.dtype)


def add_norm(x, delta, g, *, write_x=True, out_dtype=BF16):
    m, d = x.shape
    row = pl.BlockSpec((TM_NORM, d), lambda i: (i, 0))
    ins = [x] + ([delta] if delta is not None else []) + [g.reshape(1, d)]
    in_specs = [row] * (len(ins) - 1) + [pl.BlockSpec((1, d), lambda i: (0, 0))]
    out_shape = ([jax.ShapeDtypeStruct((m, d), F32)] if write_x else []) + [jax.ShapeDtypeStruct((m, d), out_dtype)]
    outs = pl.pallas_call(
        functools.partial(_norm_kernel, has_delta=delta is not None, write_x=write_x),
        grid=(m // TM_NORM,), in_specs=in_specs, out_specs=[row] * len(out_shape), out_shape=out_shape,
        compiler_params=_cp(1), name="add_norm")(*ins)
    return (outs[0], outs[1]) if write_x else (None, outs[0])


def _ffn_kernel(h_ref, wg_ref, wu_ref, wo_ref, o_ref):
    @pl.when(pl.program_id(1) == 0)
    def _():
        o_ref[...] = jnp.zeros(o_ref.shape, F32)

    h = h_ref[...]
    g = _dot(h, wg_ref[...].astype(BF16))
    u = _dot(h, wu_ref[...].astype(BF16))
    a = (0.5 * _silu(g) * u).astype(BF16)
    for c0 in range(0, o_ref.shape[1], FFN_OUT_SLAB):
        cols = slice(c0, c0 + FFN_OUT_SLAB)
        o_ref[:, cols] += _dot(a, wo_ref[:, cols].astype(BF16))


def ffn_half(h, w_in, w_out, layer):
    m, d = h.shape
    nf = D_FF // FFN_CHUNK
    once = pl.Buffered(1)
    return pl.pallas_call(
        _ffn_kernel, grid=(m // TM, nf),
        in_specs=[pl.BlockSpec((TM, d), lambda i, f: (i, 0), pipeline_mode=once),
                  pl.BlockSpec((None, d, FFN_CHUNK), lambda i, f: (layer, 0, f)),
                  pl.BlockSpec((None, d, FFN_CHUNK), lambda i, f: (layer, 0, f + nf)),
                  pl.BlockSpec((None, FFN_CHUNK, d), lambda i, f: (layer, f, 0))],
        out_specs=pl.BlockSpec((TM, d), lambda i, f: (i, 0), pipeline_mode=once),
        out_shape=jax.ShapeDtypeStruct((m, d), F32), compiler_params=_cp(2), name="ffn")(h, w_in, w_in, w_out)


def _resident_bf16(w_ref, wb_ref):
    @pl.when(pl.program_id(1) == 0)
    def _():
        wb_ref[...] = w_ref[...].astype(BF16)

    return wb_ref[...]


def _mm_kernel(x_ref, w_ref, o_ref, *scratch, act):
    w = _resident_bf16(w_ref, scratch[0]) if scratch else w_ref[...]
    acc = _dot(x_ref[...], w)
    if act == "sigmoid":
        acc = jax.nn.sigmoid(acc)
    o_ref[...] = acc.astype(o_ref.dtype)


def _layer_block(shape, layer, index):
    return pl.BlockSpec((None,) + shape, lambda j, i: (layer,) + index(j, i))


def matmul(x, w, *, tn, layer=None, act=None, out_dtype=F32):
    m, k = x.shape
    n = w.shape[-1]
    if layer is None:
        w_spec, scratch = pl.BlockSpec((k, tn), lambda j, i: (0, j)), []
    else:
        w_spec, scratch = _layer_block((k, tn), layer, lambda j, i: (0, j)), [pltpu.VMEM((k, tn), BF16)]
    return pl.pallas_call(
        functools.partial(_mm_kernel, act=act), grid=(n // tn, m // TM),
        in_specs=[pl.BlockSpec((TM, k), lambda j, i: (i, 0)), w_spec],
        out_specs=pl.BlockSpec((TM, tn), lambda j, i: (i, j)), scratch_shapes=scratch,
        out_shape=jax.ShapeDtypeStruct((m, n), out_dtype), compiler_params=_cp(2), name="matmul")(x, w)


def _ple_kernel(h_ref, p_ref, wg_ref, wp_ref, o_ref, wgb_ref):
    gate = jax.nn.sigmoid(_dot(h_ref[...], _resident_bf16(wg_ref, wgb_ref)))
    o_ref[...] = gate * _dot(p_ref[...], wp_ref[...].astype(BF16))


def ple_delta(h, ple, w_gate, w_pe, layer, *, tn=512):
    m, k = h.shape
    n = w_gate.shape[-1]
    return pl.pallas_call(
        _ple_kernel, grid=(n // tn, m // TM),
        in_specs=[pl.BlockSpec((TM, k), lambda j, i: (i, 0)), pl.BlockSpec((TM, PLE_DIM), lambda j, i: (i, 0)),
                  _layer_block((k, tn), layer, lambda j, i: (0, j)),
                  _layer_block((PLE_DIM, tn), layer, lambda j, i: (0, j))],
        out_specs=pl.BlockSpec((TM, tn), lambda j, i: (i, j)), scratch_shapes=[pltpu.VMEM((k, tn), BF16)],
        out_shape=jax.ShapeDtypeStruct((m, n), F32), compiler_params=_cp(2), name="ple")(h, ple, w_gate, w_pe)


def _merge_kernel(*refs):
    o_refs, w_refs, g_refs, out_ref, wb_refs = refs[0:4], refs[4:8], refs[8:12], refs[12], refs[13:17]
    acc = None
    for o_ref, w_ref, g_ref, wb_ref in zip(o_refs, w_refs, g_refs, wb_refs):
        term = g_ref[...].astype(F32) * _dot(o_ref[...], _resident_bf16(w_ref, wb_ref))
        acc = term if acc is None else acc + term
    out_ref[...] = acc.astype(out_ref.dtype)


def merge(branches, weights, gates, layer, *, tn=512):
    m = branches[0].shape[0]
    nb = D_MODEL // tn
    in_specs = ([pl.BlockSpec((TM, W_MIX), lambda j, i: (i, 0))] * 4
                + [_layer_block((W_MIX, tn), layer, lambda j, i: (0, j))] * 4
                + [pl.BlockSpec((TM, tn), lambda j, i, b=b: (i, b * nb + j)) for b in range(4)])
    return pl.pallas_call(
        _merge_kernel, grid=(nb, m // TM), in_specs=in_specs,
        out_specs=pl.BlockSpec((TM, tn), lambda j, i: (i, j)), scratch_shapes=[pltpu.VMEM((W_MIX, tn), BF16)] * 4,
        out_shape=jax.ShapeDtypeStruct((m, D_MODEL), BF16), compiler_params=_cp(2), name="merge")(
            *branches, *weights, gates, gates, gates, gates)


CONV_HALO = 32


def _conv_kernel(val_ref, gate_ref, halo_ref, w_ref, cb_ref, lg_ref, lb_ref, o_ref, st_ref, aext, *, tt, rb, n_last):
    t = pl.program_id(1)

    @pl.when(t == 0)
    def _():
        aext[0:CONV_HALO, :] = halo_ref[...]

    aext[CONV_HALO:CONV_HALO + tt, :] = val_ref[...] * jax.nn.sigmoid(gate_ref[...])
    first = CONV_HALO - (CONV_WIDTH - 1)
    for r0 in range(0, tt, rb):
        acc = jnp.zeros((rb, W_MIX), F32) + cb_ref[...]
        for j in range(CONV_WIDTH):
            acc = acc + aext[r0 + first + j:r0 + first + j + rb, :] * w_ref[j:j + 1, :]
        y = _silu(_layernorm(acc, lg_ref[...], lb_ref[...]))
        o_ref[r0:r0 + rb, :] = y.astype(o_ref.dtype)

    @pl.when(t == pl.num_programs(1) - 1)
    def _():
        st_ref[...] = aext[n_last:n_last + CONV_HALO, :]

    if tt >= CONV_HALO:
        aext[0:CONV_HALO, :] = aext[tt:tt + CONV_HALO, :]


def conv_mixer(z, halo, conv_w, conv_b, ln_g, ln_b, *, nb, t_len, tt, rb, n_last):
    nt = t_len // tt
    w = jnp.zeros((CONV_HALO, W_MIX), F32).at[:CONV_WIDTH].set(conv_w)
    vec = pl.BlockSpec((1, W_MIX), lambda b, t: (0, 0))
    return pl.pallas_call(
        functools.partial(_conv_kernel, tt=tt, rb=rb, n_last=n_last), grid=(nb, nt),
        in_specs=[pl.BlockSpec((tt, W_MIX), lambda b, t: (b * nt + t, C_VAL // W_MIX)),
                  pl.BlockSpec((tt, W_MIX), lambda b, t: (b * nt + t, C_GATE // W_MIX)),
                  pl.BlockSpec((None, CONV_HALO, W_MIX), lambda b, t: (b, 0, 0)),
                  pl.BlockSpec((CONV_HALO, W_MIX), lambda b, t: (0, 0)), vec, vec, vec],
        out_specs=[pl.BlockSpec((tt, W_MIX), lambda b, t: (b * nt + t, 0)),
                   pl.BlockSpec((None, CONV_HALO, W_MIX), lambda b, t: (b, 0, 0))],
        out_shape=[jax.ShapeDtypeStruct((nb * t_len, W_MIX), BF16), jax.ShapeDtypeStruct((nb, CONV_HALO, W_MIX), F32)],
        scratch_shapes=[pltpu.VMEM((CONV_HALO + tt, W_MIX), F32)],
        compiler_params=_cp(2), name="conv")(z, z, halo, w, conv_b.reshape(1, -1), ln_g.reshape(1, -1), ln_b.reshape(1, -1))


def _sgu_kernel(u_ref, v_ref, wc_ref, bs_ref, lg_ref, lb_ref, o_ref, *, tt):
    for c0 in range(0, tt, SGU_CHUNK):
        rows = slice(c0, c0 + SGU_CHUNK)
        u = jax.nn.gelu(u_ref[rows, :])
        v = _layernorm(jax.nn.gelu(v_ref[rows, :]), lg_ref[...], lb_ref[...]).astype(BF16)
        parts = [_dot(wc_ref[g], v[:, g * LANES:(g + 1) * LANES]) for g in range(SGU_GROUPS)]
        mix = jnp.concatenate(parts, axis=1) + bs_ref[...]
        o_ref[rows, :] = (u * mix).astype(o_ref.dtype)


def sgu_mixer(z, w_causal, bias_rows, ln_g, ln_b, *, rows, tt=256):
    vec = pl.BlockSpec((1, W_MIX), lambda i: (0, 0))
    return pl.pallas_call(
        functools.partial(_sgu_kernel, tt=tt), grid=(rows // tt,),
        in_specs=[pl.BlockSpec((tt, W_MIX), lambda i: (i, C_U // W_MIX)),
                  pl.BlockSpec((tt, W_MIX), lambda i: (i, C_V // W_MIX)),
                  pl.BlockSpec((SGU_GROUPS, SGU_CHUNK, SGU_CHUNK), lambda i: (0, 0, 0)),
                  pl.BlockSpec((SGU_CHUNK, W_MIX), lambda i: (0, 0)), vec, vec],
        out_specs=pl.BlockSpec((tt, W_MIX), lambda i: (i, 0)),
        out_shape=jax.ShapeDtypeStruct((rows, W_MIX), BF16), compiler_params=_cp(1), name="sgu")(
            z, z, w_causal, bias_rows, ln_g.reshape(1, -1), ln_b.reshape(1, -1))


def _sgu_step_kernel(u_ref, v_ref, w0_ref, b0_ref, lg_ref, lb_ref, o_ref, vo_ref):
    u = jax.nn.gelu(u_ref[...])
    v = _layernorm(jax.nn.gelu(v_ref[...]), lg_ref[...], lb_ref[...])
    vo_ref[...] = v
    o_ref[...] = (u * (v * w0_ref[...] + b0_ref[...])).astype(o_ref.dtype)


def sgu_step(zs, w0, b0, ln_g, ln_b):
    n = zs.shape[0]
    vec = pl.BlockSpec((1, W_MIX), lambda i: (0, 0))
    return pl.pallas_call(
        _sgu_step_kernel, grid=(1,),
        in_specs=[pl.BlockSpec((n, W_MIX), lambda i: (0, C_U // W_MIX)),
                  pl.BlockSpec((n, W_MIX), lambda i: (0, C_V // W_MIX)), vec, vec, vec, vec],
        out_specs=[pl.BlockSpec((n, W_MIX), lambda i: (0, 0))] * 2,
        out_shape=[jax.ShapeDtypeStruct((n, W_MIX), BF16), jax.ShapeDtypeStruct((n, W_MIX), F32)],
        compiler_params=_cp(1), name="sgu_step")(zs, zs, w0, b0, ln_g.reshape(1, -1), ln_b.reshape(1, -1))


def _log_sigmoid(x):
    return jnp.minimum(x, 0.0) - jnp.log1p(jnp.exp(-jnp.abs(x)))


def _gla_kernel(q_ref, k_ref, v_ref, r_ref, da_ref, wa_ref, ba_ref, gn_ref, s0_ref, o_ref, so_ref, st, *, tt, chunk, n_valid):
    t = pl.program_id(1)

    @pl.when(t == 0)
    def _():
        st[...] = s0_ref[...]

    row = lax.broadcasted_iota(jnp.int32, (chunk, chunk), 0)
    col = lax.broadcasted_iota(jnp.int32, (chunk, chunk), 1)
    causal = row >= col
    tril = causal.astype(BF16)
    for c0 in range(0, tt, chunk):
        rows = slice(c0, c0 + chunk)
        la = _log_sigmoid(_dot(da_ref[rows, :].astype(BF16), wa_ref[...]) + ba_ref[...]) * (1.0 / GLA_TAU)
        if n_valid < chunk:
            la = jnp.where(lax.broadcasted_iota(jnp.int32, la.shape, 0) < n_valid, la, 0.0)
        b = _split_dot_left(tril, la)
        bl = b[chunk - 1:chunk, :]
        kk = k_ref[rows, :]
        q_in = (q_ref[rows, :] * (GLA_DK ** -0.5) * jnp.exp(b)).astype(BF16)
        k_in = (kk * jnp.exp(-b)).astype(BF16)
        k_end = (kk * jnp.exp(bl - b)).astype(BF16)
        decay = jnp.exp(bl)
        vv = v_ref[rows, :].astype(BF16)
        for h in range(GLA_HEADS):
            hs = slice(h * LANES, (h + 1) * LANES)
            a = jnp.where(causal, _dot_nt(q_in[:, hs], k_in[:, hs]), 0.0).astype(BF16)
            o = _dot(a, vv[:, hs]) + _dot_nt(q_in[:, hs], st[h].astype(BF16))
            st[h] = st[h] * decay[:, hs] + _dot_tn(vv[:, hs], k_end[:, hs])
            y = o * lax.rsqrt(jnp.mean(o * o, axis=-1, keepdims=True) + EPS) * gn_ref[...]
            o_ref[rows, hs] = (y * _silu(r_ref[rows, hs])).astype(o_ref.dtype)

    @pl.when(t == pl.num_programs(1) - 1)
    def _():
        so_ref[...] = st[...]


def _split_dot_left(w, x):
    hi = x.astype(BF16)
    lo = (x - hi.astype(F32)).astype(BF16)
    return _dot(w, hi) + _dot(w, lo)


def gla_mixer(z, s0_t, w_a2p, b_ap, g_norm, *, nb, t_len, tt, chunk, n_valid):
    nt = t_len // tt
    rowspec = lambda c: pl.BlockSpec((tt, W_MIX), lambda b, t: (b * nt + t, c // W_MIX))
    state = pl.BlockSpec((None, GLA_HEADS, GLA_DV, LANES), lambda b, t: (b, 0, 0, 0))
    return pl.pallas_call(
        functools.partial(_gla_kernel, tt=tt, chunk=chunk, n_valid=n_valid), grid=(nb, nt),
        in_specs=[rowspec(C_DQ), rowspec(C_DK), rowspec(C_DV), rowspec(C_DR),
                  pl.BlockSpec((tt, LANES), lambda b, t: (b * nt + t, C_DA // LANES)),
                  pl.BlockSpec((LANES, W_MIX), lambda b, t: (0, 0)),
                  pl.BlockSpec((1, W_MIX), lambda b, t: (0, 0)),
                  pl.BlockSpec((1, LANES), lambda b, t: (0, 0)), state],
        out_specs=[pl.BlockSpec((tt, W_MIX), lambda b, t: (b * nt + t, 0)), state],
        out_shape=[jax.ShapeDtypeStruct((nb * t_len, W_MIX), BF16),
                   jax.ShapeDtypeStruct((nb, GLA_HEADS, GLA_DV, LANES), F32)],
        scratch_shapes=[pltpu.VMEM((GLA_HEADS, GLA_DV, LANES), F32)],
        compiler_params=_cp(2), name="gla")(z, z, z, z, z, w_a2p, b_ap, g_norm.reshape(1, -1), s0_t)


SEG_PER_STEP = 128
CMP_ROWS = SEG_PER_STEP * CMP_STRIDE


def _compress_kernel(*refs, n_in, tail_from_next):
    pos = 1 if tail_from_next else 0
    in_refs = refs[pos:pos + n_in]
    w_ref, o_ref = refs[pos + n_in], refs[pos + n_in + 1]
    xs = refs[pos + n_in + 2:]
    n_main = n_in - 1 if tail_from_next else n_in
    rows_each = CMP_ROWS // n_main
    for ch in range(4):
        cols = slice(ch * LANES, (ch + 1) * LANES)
        for j in range(n_main):
            xs[ch][j * rows_each:(j + 1) * rows_each, :] = in_refs[j][:, cols]
        if tail_from_next:
            xs[ch][CMP_ROWS:CMP_ROWS + CMP_STRIDE, :] = in_refs[n_in - 1][0:CMP_STRIDE, cols]
        else:
            xs[ch][CMP_ROWS:CMP_ROWS + CMP_STRIDE, :] = jnp.zeros((CMP_STRIDE, LANES), F32)
        acc = jnp.zeros((SEG_PER_STEP, LANES), F32)
        for r in range(CMP_LEN):
            x = xs[ch][pl.ds(r, SEG_PER_STEP, stride=CMP_STRIDE), :]
            acc = acc + _dot(x.astype(BF16), w_ref[ch // 2, r])
        o_ref[:, cols] = acc


def compress_prompt(z, w_cmp):
    return pl.pallas_call(
        functools.partial(_compress_kernel, n_in=1, tail_from_next=False), grid=(BATCH,),
        in_specs=[pl.BlockSpec((CMP_ROWS, 4 * LANES), lambda b: (b, C_CMP // (4 * LANES))),
                  pl.BlockSpec((2, CMP_LEN, LANES, LANES), lambda b: (0, 0, 0, 0))],
        out_specs=pl.BlockSpec((None, SEG_PER_STEP, 4 * LANES), lambda b: (b, 0, 0)),
        out_shape=jax.ShapeDtypeStruct((BATCH, SEG_PER_STEP, 4 * LANES), F32),
        scratch_shapes=[pltpu.VMEM((CMP_ROWS + CMP_STRIDE, LANES), F32)] * 4,
        compiler_params=_cp(1), name="compress_prompt")(z, w_cmp)


PAGES_PER_STEP = CMP_ROWS // PAGE_SIZE
N_PAGES = PAST_LEN // PAGE_SIZE


def compress_paged(cache, page_table, w_cmp):
    n_steps = N_PAGES // PAGES_PER_STEP

    def page_spec(j):
        return pl.BlockSpec((None, PAGE_SIZE, 4 * LANES),
                            lambda b, c, pt: (pt[b, jnp.minimum(c * PAGES_PER_STEP + j, N_PAGES - 1)], 0, 0))

    grid_spec = pltpu.PrefetchScalarGridSpec(
        num_scalar_prefetch=1, grid=(DEC_BATCH, n_steps),
        in_specs=[page_spec(j) for j in range(PAGES_PER_STEP + 1)]
        + [pl.BlockSpec((2, CMP_LEN, LANES, LANES), lambda b, c, pt: (0, 0, 0, 0))],
        out_specs=pl.BlockSpec((None, SEG_PER_STEP, 4 * LANES), lambda b, c, pt: (b, c, 0)),
        scratch_shapes=[pltpu.VMEM((CMP_ROWS + CMP_STRIDE, LANES), F32)] * 4)
    return pl.pallas_call(
        functools.partial(_compress_kernel, n_in=PAGES_PER_STEP + 1, tail_from_next=True), grid_spec=grid_spec,
        out_shape=jax.ShapeDtypeStruct((DEC_BATCH, n_steps * SEG_PER_STEP, 4 * LANES), F32),
        compiler_params=_cp(2), name="compress_paged")(page_table, *([cache] * (PAGES_PER_STEP + 1)), w_cmp)


def _top_select(score, n_lanes):
    lane = lax.broadcasted_iota(jnp.int32, score.shape, 1).astype(F32)
    sel = jnp.zeros(score.shape, F32)
    for _ in range(N_SELECT):
        m = jnp.max(score, axis=-1, keepdims=True)
        idx = jnp.min(jnp.where(score == m, lane, float(n_lanes)), axis=-1, keepdims=True)
        pick = lane == idx
        sel = jnp.where(pick, 1.0, sel)
        score = jnp.where(pick, -3e38, score)
    return sel


def _bias_tiles(delta_of_tile, n_tiles, t0, t1, t2):
    tiles = []
    for j in range(n_tiles):
        delta = delta_of_tile(j)
        tiles.append(jnp.where(delta == 0, t0, jnp.where(delta == 1, t1, t2)))
    return jnp.concatenate(tiles, axis=1)


WIN_BAND = WINDOW + Q_BLOCK


SLC_KEY_STEP = 512


def _nsa_prompt_kernel(q_ref, cbk_ref, cbv_ref, ks_ref, vs_ref, kw_ref, vw_ref, g_ref, bc_ref, t0_ref, t1_ref, t2_ref,
                       covt_ref, e_ref, o_ref, oslc_ref):
    h = pl.program_id(1)
    i = pl.program_id(2)
    q = q_ref[...] * (HEAD_DIM ** -0.5)
    qs = jnp.concatenate([q[:, g * LANES:(g + 1) * LANES] for g in range(GRP)], axis=0).astype(BF16)
    qpos = i * Q_BLOCK + lax.broadcasted_iota(jnp.int32, (Q_BLOCK, 1), 0)
    lane = lax.broadcasted_iota(jnp.int32, (Q_BLOCK, LANES), 1)

    sc = _dot_nt(qs, cbk_ref[...].astype(BF16))
    cmask = (lane * CMP_STRIDE + (CMP_LEN - 1) <= qpos) & (lane < SEQ // CMP_STRIDE - 1)
    p_c = [_masked_softmax(sc[g * Q_BLOCK:(g + 1) * Q_BLOCK] + bc_ref[g], cmask) for g in range(GRP)]
    o_cmp = _dot(jnp.concatenate(p_c, axis=0).astype(BF16), cbv_ref[...].astype(BF16))
    psum = p_c[0] + p_c[1] + p_c[2] + p_c[3]
    p_hi = psum.astype(BF16)
    p_lo = (psum - p_hi.astype(F32)).astype(BF16)
    imp_t = _dot_nt(covt_ref[...], p_hi) + _dot_nt(covt_ref[...], p_lo)

    n_blk = SEQ // SLC_LEN
    blk = lax.broadcasted_iota(jnp.int32, (n_blk, Q_BLOCK), 0)
    qpos_l = i * Q_BLOCK + lax.broadcasted_iota(jnp.int32, (n_blk, Q_BLOCK), 1)
    cur = lax.shift_right_logical(qpos_l, 6)
    valid = blk * SLC_LEN <= qpos_l
    forced = (blk == 0) | (blk == cur) | (blk == cur - 1)
    score = jnp.where(valid, imp_t[0:n_blk] + jnp.where(forced, FORCE_BONUS, 0.0), NEG)
    rank = jnp.zeros((n_blk, Q_BLOCK), F32)
    for other_blk in range(n_blk):
        other = score[other_blk:other_blk + 1, :]
        rank = rank + jnp.where((other > score) | ((other == score) & (blk > other_blk)), 1.0, 0.0)
    sel_t = jnp.where(valid & (rank < N_SELECT), 1.0, 0.0).astype(BF16)

    def selected(n_keys):
        kpos = lax.broadcasted_iota(jnp.int32, (Q_BLOCK, n_keys), 1)
        smask = (_dot_tn(sel_t, e_ref[0:n_blk, 0:n_keys]) > 0.5) & (kpos <= qpos)
        ss = _dot_nt(qs, ks_ref[0:n_keys, :].astype(BF16))
        p_s = []
        for g in range(GRP):
            bias = _bias_tiles(lambda j: i - j, n_keys // Q_BLOCK, t0_ref[g], t1_ref[g], t2_ref[g])
            p_s.append(_masked_softmax(ss[g * Q_BLOCK:(g + 1) * Q_BLOCK] + bias, smask))
        oslc_ref[...] = _dot(jnp.concatenate(p_s, axis=0).astype(BF16), vs_ref[0:n_keys, :].astype(BF16))

    tiles_per_step = SLC_KEY_STEP // Q_BLOCK
    for hi in range(tiles_per_step, SEQ // Q_BLOCK + 1, tiles_per_step):
        pl.when((i >= hi - tiles_per_step) & (i < hi))(functools.partial(selected, hi * Q_BLOCK))
    o_slc = oslc_ref[...]

    first_tile = jnp.maximum(i - WINDOW // Q_BLOCK, 0)
    start = pl.multiple_of(first_tile * Q_BLOCK, Q_BLOCK)
    dist = qpos - (start + lax.broadcasted_iota(jnp.int32, (Q_BLOCK, WIN_BAND), 1))
    wmask = (dist >= 0) & (dist <= WINDOW)
    sw = _dot_nt(qs, kw_ref[pl.ds(start, WIN_BAND), :].astype(BF16))
    p_w = []
    for g in range(GRP):
        bias = _bias_tiles(lambda j: i - first_tile - j, WIN_BAND // Q_BLOCK, t0_ref[g], t1_ref[g], t2_ref[g])
        p_w.append(_masked_softmax(sw[g * Q_BLOCK:(g + 1) * Q_BLOCK] + bias, wmask))
    o_win = _dot(jnp.concatenate(p_w, axis=0).astype(BF16), vw_ref[pl.ds(start, WIN_BAND), :].astype(BF16))

    gates = jax.nn.sigmoid(g_ref[...])
    for g in range(GRP):
        rows = slice(g * Q_BLOCK, (g + 1) * Q_BLOCK)
        base = (h * GRP + g) * 3
        gc = [jnp.sum(jnp.where(lane == base + br, gates, 0.0), axis=-1, keepdims=True) for br in range(3)]
        o = gc[0] * o_cmp[rows] + gc[1] * o_slc[rows] + gc[2] * o_win[rows]
        o_ref[:, g * LANES:(g + 1) * LANES] = o.astype(o_ref.dtype)


def nsa_prompt(z, cb, bias_c, t0, t1, t2, cover_t, expand):
    nq = SEQ // Q_BLOCK
    kv = lambda c: pl.BlockSpec((SEQ, LANES), lambda b, h, i, c=c: (b, c // LANES + h))
    cbs = lambda c: pl.BlockSpec((None, SEG_PER_STEP, LANES), lambda b, h, i, c=c: (b, 0, c + h))
    toe = pl.BlockSpec((GRP, Q_BLOCK, LANES), lambda b, h, i: (h, 0, 0))
    return pl.pallas_call(
        _nsa_prompt_kernel, grid=(BATCH, KVH, nq),
        in_specs=[pl.BlockSpec((Q_BLOCK, GRP * LANES), lambda b, h, i: (b * nq + i, C_Q // (GRP * LANES) + h)),
                  cbs(0), cbs(KVH), kv(C_SLC), kv(C_SLC + KVH * LANES), kv(C_WIN), kv(C_WIN + KVH * LANES),
                  pl.BlockSpec((Q_BLOCK, LANES), lambda b, h, i: (b * nq + i, C_CG // LANES)),
                  pl.BlockSpec((GRP, Q_BLOCK, LANES), lambda b, h, i: (h, i, 0)), toe, toe, toe,
                  pl.BlockSpec((LANES, LANES), lambda b, h, i: (0, 0)),
                  pl.BlockSpec((LANES, SEQ), lambda b, h, i: (0, 0))],
        out_specs=pl.BlockSpec((Q_BLOCK, GRP * LANES), lambda b, h, i: (b * nq + i, h)),
        scratch_shapes=[pltpu.VMEM((GRP * Q_BLOCK, LANES), F32)],
        out_shape=jax.ShapeDtypeStruct((M_PROMPT, W_MIX), BF16), compiler_params=_cp(3), name="nsa_prompt")(
            z, cb, cb, z, z, z, z, z, bias_c, t0, t1, t2, cover_t, expand)


N_CMP_S = PAST_LEN // CMP_STRIDE
N_SLC_S = 384


def _head_rows(shape):
    return lax.broadcasted_iota(jnp.int32, shape, 0) < GRP


def _nsa_step_a_kernel(q_ref, cb_ref, bc_ref, cov_ref, wb_ref, kvn_ref, bw_ref, bwn_ref, ocw_ref, sel_ref):
    q = q_ref[...] * (HEAD_DIM ** -0.5)
    qb = q.astype(BF16)
    cb = cb_ref[...]
    first = _head_rows((NSA_HEADS, N_CMP_S))
    s = jnp.where(first, _dot_nt(qb, cb[:, 0:LANES].astype(BF16)), _dot_nt(qb, cb[:, LANES:2 * LANES].astype(BF16)))
    kidx = lax.broadcasted_iota(jnp.int32, (NSA_HEADS, N_CMP_S), 1)
    p = _masked_softmax(s + bc_ref[...], kidx * CMP_STRIDE + (CMP_LEN - 1) <= PAST_LEN)
    pb = p.astype(BF16)
    o_cmp = jnp.where(_head_rows((NSA_HEADS, LANES)), _dot(pb, cb[:, 2 * LANES:3 * LANES].astype(BF16)),
                      _dot(pb, cb[:, 3 * LANES:4 * LANES].astype(BF16)))
    row = lax.broadcasted_iota(jnp.int32, (NSA_HEADS, N_CMP_S), 0)
    ps0 = jnp.sum(jnp.where(first, p, 0.0), axis=0, keepdims=True)
    ps1 = jnp.sum(jnp.where(first, 0.0, p), axis=0, keepdims=True)
    psum = jnp.where(row == 0, ps0, jnp.where(row == 1, ps1, 0.0))
    imp = _split_dot(psum, cov_ref[...])
    lane = lax.broadcasted_iota(jnp.int32, (NSA_HEADS, N_SLC_S), 1)
    cur = PAST_LEN // SLC_LEN
    valid = lane <= cur
    forced = (lane == 0) | (lane == cur) | (lane == cur - 1)
    score = jnp.where(valid, imp + jnp.where(forced, FORCE_BONUS, 0.0), NEG)
    sel_ref[...] = _top_select(score, N_SLC_S) * valid.astype(F32)

    wb = wb_ref[...]
    firstw = _head_rows((NSA_HEADS, WINDOW))
    sw = jnp.where(firstw, _dot_nt(qb, wb[:, 0:LANES].astype(BF16)), _dot_nt(qb, wb[:, LANES:2 * LANES].astype(BF16)))
    sw = sw + bw_ref[...]
    firstl = _head_rows((NSA_HEADS, LANES))
    kvn = kvn_ref[...]
    k_new = jnp.where(firstl, kvn[:, 0:LANES], kvn[:, LANES:2 * LANES])
    v_new = jnp.where(firstl, kvn[:, 2 * LANES:3 * LANES], kvn[:, 3 * LANES:4 * LANES])
    s_new = jnp.sum(q * k_new, axis=-1, keepdims=True) + bwn_ref[:, 0:1]
    m = jnp.maximum(jnp.max(sw, axis=-1, keepdims=True), s_new)
    pw = jnp.exp(sw - m)
    pn = jnp.exp(s_new - m)
    pwb = pw.astype(BF16)
    acc = jnp.where(firstl, _dot(pwb, wb[:, 2 * LANES:3 * LANES].astype(BF16)), _dot(pwb, wb[:, 3 * LANES:4 * LANES].astype(BF16)))
    o_win = (acc + pn * v_new) / (jnp.sum(pw, axis=-1, keepdims=True) + pn)
    ocw_ref[:, 0:LANES] = o_cmp
    ocw_ref[:, LANES:2 * LANES] = o_win


def nsa_step_a(q_s, cb_s, bias_c, cover, win_buf, kvw_new, bias_w, bias_wn):
    per_b = lambda *shape: pl.BlockSpec((None,) + shape, lambda b: (b,) + (0,) * len(shape))
    whole = lambda *shape: pl.BlockSpec(shape, lambda b: (0,) * len(shape))
    return pl.pallas_call(
        _nsa_step_a_kernel, grid=(DEC_BATCH,),
        in_specs=[per_b(NSA_HEADS, LANES), per_b(N_CMP_S, 4 * LANES), whole(NSA_HEADS, N_CMP_S), whole(N_CMP_S, N_SLC_S),
                  per_b(WINDOW, 4 * LANES), per_b(1, 4 * LANES), whole(NSA_HEADS, WINDOW), whole(NSA_HEADS, LANES)],
        out_specs=[per_b(NSA_HEADS, 2 * LANES), per_b(NSA_HEADS, N_SLC_S)],
        out_shape=[jax.ShapeDtypeStruct((DEC_BATCH, NSA_HEADS, 2 * LANES), F32),
                   jax.ShapeDtypeStruct((DEC_BATCH, NSA_HEADS, N_SLC_S), F32)],
        compiler_params=_cp(1), name="nsa_step_a")(q_s, cb_s, bias_c, cover, win_buf, kvw_new, bias_w, bias_wn)


def _nsa_step_b_kernel(*refs):
    pages = refs[1:1 + PAGES_PER_STEP]
    (q_ref, sel_ref, e_ref, bs_ref, kvn_ref, seln_ref, bn_ref, ocw_ref, g_ref, o_ref, m_sc, l_sc, acc_sc) = refs[1 + PAGES_PER_STEP:]
    c = pl.program_id(1)

    @pl.when(c == 0)
    def _():
        m_sc[...] = jnp.full(m_sc.shape, NEG, F32)
        l_sc[...] = jnp.zeros(l_sc.shape, F32)
        acc_sc[...] = jnp.zeros(acc_sc.shape, F32)

    q = q_ref[...] * (HEAD_DIM ** -0.5)
    qb = q.astype(BF16)
    kv = [jnp.concatenate([pg[:, j * LANES:(j + 1) * LANES].astype(BF16) for pg in pages], axis=0) for j in range(4)]
    first = _head_rows((NSA_HEADS, CMP_ROWS))
    firstl = _head_rows((NSA_HEADS, LANES))
    s = jnp.where(first, _dot_nt(qb, kv[0]), _dot_nt(qb, kv[1])) + bs_ref[...]
    mask = _dot(sel_ref[...].astype(BF16), e_ref[...]) > 0.5
    s = jnp.where(mask, s, NEG)
    m_old = m_sc[...]
    m_new = jnp.maximum(m_old, jnp.max(s, axis=-1, keepdims=True))
    alpha = jnp.exp(m_old - m_new)
    p = jnp.where(mask, jnp.exp(s - m_new), 0.0)
    pb = p.astype(BF16)
    l_sc[...] = alpha * l_sc[...] + jnp.sum(p, axis=-1, keepdims=True)
    acc_sc[...] = alpha * acc_sc[...] + jnp.where(firstl, _dot(pb, kv[2]), _dot(pb, kv[3]))
    m_sc[...] = m_new

    @pl.when(c == pl.num_programs(1) - 1)
    def _():
        kvn = kvn_ref[...]
        k_new = jnp.where(firstl, kvn[:, 0:LANES], kvn[:, LANES:2 * LANES])
        v_new = jnp.where(firstl, kvn[:, 2 * LANES:3 * LANES], kvn[:, 3 * LANES:4 * LANES])
        on = seln_ref[:, 0:1] > 0.5
        s_n = jnp.where(on, jnp.sum(q * k_new, axis=-1, keepdims=True) + bn_ref[:, 0:1], NEG)
        m_old = m_sc[...]
        m_fin = jnp.maximum(m_old, s_n)
        alpha = jnp.exp(m_old - m_fin)
        p_n = jnp.where(on, jnp.exp(s_n - m_fin), 0.0)
        l_fin = alpha * l_sc[...] + p_n
        o_slc = (alpha * acc_sc[...] + p_n * v_new) / jnp.maximum(l_fin, 1e-30)
        gates = jax.nn.sigmoid(g_ref[...])
        o = gates[:, 0:1] * ocw_ref[:, 0:LANES] + gates[:, 1:2] * o_slc + gates[:, 2:3] * ocw_ref[:, LANES:2 * LANES]
        o_ref[...] = o.astype(o_ref.dtype)


def nsa_step_b(cache, page_table, q_s, sel_r, expand, bias_s, kvs_new, sel_new, bias_n, ocw, gates):
    n_steps = N_PAGES // PAGES_PER_STEP

    def page_spec(j):
        return pl.BlockSpec((None, PAGE_SIZE, 4 * LANES), lambda b, c, pt: (pt[b, c * PAGES_PER_STEP + j], 0, 0))

    per_b = lambda *shape: pl.BlockSpec((None,) + shape, lambda b, c, pt: (b,) + (0,) * len(shape))
    whole = lambda *shape: pl.BlockSpec(shape, lambda b, c, pt: (0,) * len(shape))
    grid_spec = pltpu.PrefetchScalarGridSpec(
        num_scalar_prefetch=1, grid=(DEC_BATCH, n_steps),
        in_specs=[page_spec(j) for j in range(PAGES_PER_STEP)] + [
            per_b(NSA_HEADS, LANES),
            pl.BlockSpec((None, None, NSA_HEADS, LANES), lambda b, c, pt: (b, c, 0, 0)),
            whole(LANES, CMP_ROWS),
            pl.BlockSpec((NSA_HEADS, CMP_ROWS), lambda b, c, pt: (0, c)),
            per_b(1, 4 * LANES), per_b(NSA_HEADS, LANES), whole(NSA_HEADS, LANES),
            per_b(NSA_HEADS, 2 * LANES), per_b(NSA_HEADS, LANES)],
        out_specs=per_b(NSA_HEADS, LANES),
        scratch_shapes=[pltpu.VMEM((NSA_HEADS, 1), F32), pltpu.VMEM((NSA_HEADS, 1), F32), pltpu.VMEM((NSA_HEADS, LANES), F32)])
    return pl.pallas_call(
        _nsa_step_b_kernel, grid_spec=grid_spec,
        out_shape=jax.ShapeDtypeStruct((DEC_BATCH, NSA_HEADS, LANES), BF16),
        compiler_params=_cp(2), name="nsa_step_b")(
            page_table, *([cache] * PAGES_PER_STEP), q_s, sel_r, expand, bias_s, kvs_new, sel_new, bias_n, ocw, gates)


def _rel_bias_of(rel_bias, dist):
    n = jnp.maximum(dist, 0)
    max_exact = REL_BUCKETS // 2
    nf = jnp.maximum(n, 1).astype(F32)
    large = max_exact + (jnp.log(nf / max_exact) / math.log(REL_MAX_DIST / max_exact)
                         * (REL_BUCKETS - max_exact)).astype(jnp.int32)
    bucket = jnp.where(n < max_exact, n, jnp.minimum(large, REL_BUCKETS - 1))[None]
    out = jnp.zeros((NSA_HEADS,) + dist.shape, F32)
    for k in range(REL_BUCKETS):
        out = jnp.where(bucket == k, rel_bias[k].reshape((NSA_HEADS,) + (1,) * dist.ndim), out)
    return out


def _nsa_tables(rel_bias):
    bias_of = functools.partial(_rel_bias_of, rel_bias)
    r = jnp.arange(Q_BLOCK)
    diff = r[:, None] - r[None, :]
    t0 = bias_of(diff)
    t1 = bias_of(Q_BLOCK + diff)
    t2 = bias_of(2 * Q_BLOCK + diff)
    bias_c = bias_of(jnp.arange(SEQ)[:, None] - (jnp.arange(LANES)[None, :] * CMP_STRIDE + CMP_LEN - 1))
    k = jnp.arange(LANES)
    j = jnp.arange(LANES)
    cover = ((k[:, None] * CMP_STRIDE < (j[None, :] + 1) * SLC_LEN) & (k[:, None] * CMP_STRIDE + CMP_LEN > j[None, :] * SLC_LEN)
             & (k[:, None] < SEQ // CMP_STRIDE - 1) & (j[None, :] < SEQ // SLC_LEN)).astype(BF16)
    expand = (jnp.arange(SEQ)[None, :] // SLC_LEN == jnp.arange(LANES)[:, None]).astype(BF16)
    ks = jnp.arange(N_CMP_S)
    bias_cs = bias_of(PAST_LEN - (ks * CMP_STRIDE + CMP_LEN - 1))
    js = jnp.arange(N_SLC_S)
    cover_s = ((ks[:, None] * CMP_STRIDE < (js[None, :] + 1) * SLC_LEN) & (ks[:, None] * CMP_STRIDE + CMP_LEN > js[None, :] * SLC_LEN)
               & (ks[:, None] < N_CMP_S - 1) & (js[None, :] <= PAST_LEN // SLC_LEN)).astype(BF16)
    bias_w = bias_of(WINDOW - jnp.arange(WINDOW))
    bias_0 = bias_of(jnp.zeros((LANES,), jnp.int32))
    bias_s = bias_of(PAST_LEN - jnp.arange(PAST_LEN))
    return dict(t0=t0, t1=t1, t2=t2, bias_c=bias_c, cover_t=cover.T, expand=expand, bias_cs=bias_cs, cover_s=cover_s,
                bias_w=bias_w, bias_0=bias_0, bias_s=bias_s)


def _pad_heads(w):
    lead = w.shape[:-1]
    w = w.reshape(lead + (GLA_HEADS, GLA_DK))
    w = jnp.pad(w, [(0, 0)] * len(lead) + [(0, 0), (0, LANES - GLA_DK)])
    return w.reshape(lead + (GLA_HEADS * LANES,))


def _pad_cols(w, n):
    return jnp.pad(w, ((0, 0), (0, n - w.shape[1])))


def _repack_w_in(w):
    seg = lambda a, n: w[:, a:a + n]
    parts = [seg(0, 2048), seg(2048, 2048), seg(4096, 1024), seg(7704, 1024), seg(8744, 1024),
             _pad_heads(seg(6680, 512)), _pad_heads(seg(7192, 512)), seg(5120, 512), seg(5632, 512), seg(6144, 512),
             _pad_cols(seg(6656, 24), LANES), _pad_cols(seg(8728, 16), LANES)]
    w1 = jnp.concatenate(parts, axis=1)
    return _pad_cols(w1, ZW).astype(BF16), w[:, G_OFF:].astype(BF16)


def _mixers(i, z, tabs, wts, st):
    new = {}
    zs = z[M_PROMPT:M_PROMPT + DEC_BATCH]
    zs_pad = jnp.zeros((DEC_BATCH, 8, ZW), F32).at[:, 0].set(zs).reshape(DEC_BATCH * 8, ZW)
    tail = jnp.zeros((M_SLAB - M_PROMPT - DEC_BATCH, W_MIX), BF16)
    slab = lambda p, s: jnp.concatenate([p, s, tail], axis=0)

    zero_halo = jnp.zeros((BATCH, CONV_HALO, W_MIX), F32)
    conv_args = (wts['conv_w'], wts['conv_b'], wts['conv_ln_g'], wts['conv_ln_b'])
    oa_p, cst_p = conv_mixer(z, zero_halo, *conv_args, nb=BATCH, t_len=SEQ, tt=256, rb=32, n_last=256)
    halo_s = jnp.pad(st['conv'], ((0, 0), (CONV_HALO - (CONV_WIDTH - 1), 0), (0, 0)))
    oa_s, cst_s = conv_mixer(zs_pad, halo_s, *conv_args, nb=DEC_BATCH, t_len=8, tt=8, rb=8, n_last=1)
    new['conv_p'] = cst_p[:, CONV_HALO - (CONV_WIDTH - 1):]
    new['conv_s'] = cst_s[:, CONV_HALO - (CONV_WIDTH - 1):]
    o_a = slab(oa_p, oa_s[::8])

    ws = wts['sgu_ws']
    w_causal = (ws * jnp.tril(jnp.ones((SGU_CHUNK, SGU_CHUNK), F32))).astype(BF16)
    bias_rows = jnp.repeat(wts['sgu_bs'].T, LANES, axis=1)
    ob_p = sgu_mixer(z, w_causal, bias_rows, wts['sgu_ln_g'], wts['sgu_ln_b'], rows=M_PROMPT)
    w0 = jnp.repeat(ws[:, 0, 0], LANES)[None, :]
    ob_s, v_s = sgu_step(zs, w0, bias_rows[0:1], wts['sgu_ln_g'], wts['sgu_ln_b'])
    new['sgu_v'] = v_s[:, None, :]
    o_b = slab(ob_p, ob_s)

    w_a2p = jnp.zeros((LANES, GLA_HEADS * LANES), F32).at[:GLA_RANK].set(_pad_heads(wts['gla_w_a2'])).astype(BF16)
    b_ap = _pad_heads(wts['gla_b_a'])[None, :]
    s0_p = jnp.zeros((BATCH, GLA_HEADS, GLA_DV, LANES), F32)
    od_p, sp = gla_mixer(z, s0_p, w_a2p, b_ap, wts['gla_norm'], nb=BATCH, t_len=SEQ, tt=256, chunk=GLA_CHUNK,
                         n_valid=GLA_CHUNK)
    s0_s = jnp.pad(jnp.swapaxes(st['gla'], -1, -2), ((0, 0), (0, 0), (0, 0), (0, LANES - GLA_DK)))
    od_s, ss = gla_mixer(zs_pad, s0_s, w_a2p, b_ap, wts['gla_norm'], nb=DEC_BATCH, t_len=8, tt=8, chunk=8, n_valid=1)
    new['gla_p'] = jnp.swapaxes(sp[..., :GLA_DK], -1, -2)
    new['gla_s'] = jnp.swapaxes(ss[..., :GLA_DK], -1, -2)
    o_d = slab(od_p, od_s[::8])

    w_cmp = wts['nsa_w_cmp'].astype(BF16)
    cb_p = compress_prompt(z, w_cmp)
    oc_p = nsa_prompt(z, cb_p, tabs['bias_c'], tabs['t0'], tabs['t1'], tabs['t2'], tabs['cover_t'], tabs['expand'])
    pt = st['page_table']
    cache_cmp = st['cmp'].reshape(-1, PAGE_SIZE, 4 * LANES)
    cache_slc = st['slc'].reshape(-1, PAGE_SIZE, 4 * LANES)
    cb_s = compress_paged(cache_cmp, pt, w_cmp)
    q_s = zs[:, C_Q:C_Q + NSA_HEADS * LANES].reshape(DEC_BATCH, NSA_HEADS, LANES)
    kvw_new = zs[:, None, C_WIN:C_WIN + 4 * LANES]
    kvs_new = zs[:, None, C_SLC:C_SLC + 4 * LANES]
    win_buf = st['win'].reshape(DEC_BATCH, WINDOW, 4 * LANES)
    ocw, sel = nsa_step_a(q_s, cb_s, tabs['bias_cs'], tabs['cover_s'], win_buf, kvw_new, tabs['bias_w'], tabs['bias_0'])
    n_steps = N_PAGES // PAGES_PER_STEP
    blocks_per_step = CMP_ROWS // SLC_LEN
    sel_kv = jnp.repeat(sel[:, :KVH], GRP, axis=1)
    sel_r = sel_kv[:, :, :n_steps * blocks_per_step].reshape(DEC_BATCH, NSA_HEADS, n_steps, blocks_per_step)
    sel_r = jnp.pad(jnp.swapaxes(sel_r, 1, 2), ((0, 0), (0, 0), (0, 0), (0, LANES - blocks_per_step)))
    sel_new = jnp.broadcast_to(sel_kv[:, :, PAST_LEN // SLC_LEN, None], (DEC_BATCH, NSA_HEADS, LANES))
    gates_s = _pad_cols(zs[:, C_CG:C_CG + 3 * NSA_HEADS].reshape(DEC_BATCH * NSA_HEADS, 3), LANES).reshape(DEC_BATCH, NSA_HEADS, LANES)
    oc_s = nsa_step_b(cache_slc, pt, q_s, sel_r, tabs['expand'], tabs['bias_s'], kvs_new, sel_new, tabs['bias_0'], ocw, gates_s)
    o_c = slab(oc_p, oc_s.reshape(DEC_BATCH, W_MIX))

    kvp = lambda c: z[:M_PROMPT, c:c + 4 * LANES]
    new['cmp_p'] = kvp(C_CMP).reshape(BATCH, SEQ // PAGE_SIZE, PAGE_SIZE, 2, KVH, HEAD_DIM)
    new['slc_p'] = kvp(C_SLC).reshape(BATCH, SEQ // PAGE_SIZE, PAGE_SIZE, 2, KVH, HEAD_DIM)
    new['win_p'] = kvp(C_WIN).reshape(BATCH, SEQ, 2, KVH, HEAD_DIM)[:, SEQ - WINDOW:]
    kvs = lambda c: zs[:, c:c + 4 * LANES].reshape(DEC_BATCH, 1, 2, KVH, HEAD_DIM)
    new['cmp_s'] = kvs(C_CMP)
    new['slc_s'] = kvs(C_SLC)
    new['win_s'] = jnp.concatenate([st['win'][:, 1:], kvs(C_WIN)], axis=1)
    return (o_a, o_b, o_c, o_d), new


def kernel(x_prompt, x_sample, cache_cmp_kv, cache_slc_kv, state_win_kv, state_conv, state_gla, page_table, p_prompt, p_sample, ffn1_norm, ffn1_w_in, ffn1_w_out, mix_norm, w_in, conv_w, conv_b, conv_ln_g, conv_ln_b, conv_w_out, sgu_ln_g, sgu_ln_b, sgu_ws, sgu_bs, sgu_w_out, nsa_w_cmp, nsa_w_out, gla_w_a2, gla_b_a, gla_norm, gla_w_out, w_out, ffn2_norm, ffn2_w_in, ffn2_w_out, pe_norm, w_pe, w_pe_gate, rel_bias, final_norm):
    n_tail = M_SLAB - M_PROMPT - DEC_BATCH
    x = jnp.concatenate([x_prompt.reshape(M_PROMPT, D_MODEL), x_sample.reshape(DEC_BATCH, D_MODEL),
                         jnp.zeros((n_tail, D_MODEL), F32)], axis=0)
    tabs = _nsa_tables(rel_bias)
    delta = None
    news = []
    for i in range(DEPTH):
        x, h = add_norm(x, delta, ffn1_norm[i])
        delta = ffn_half(h, ffn1_w_in, ffn1_w_out, i)
        x, h = add_norm(x, delta, mix_norm[i])
        w1, wg = _repack_w_in(w_in[i])
        z = matmul(h, w1, tn=1024)
        gates = matmul(h, wg, tn=1024, act="sigmoid", out_dtype=BF16)
        wts = dict(conv_w=conv_w[i], conv_b=conv_b[i], conv_ln_g=conv_ln_g[i], conv_ln_b=conv_ln_b[i],
                   sgu_ws=sgu_ws[i], sgu_bs=sgu_bs[i], sgu_ln_g=sgu_ln_g[i], sgu_ln_b=sgu_ln_b[i],
                   nsa_w_cmp=nsa_w_cmp[i], gla_w_a2=gla_w_a2[i], gla_b_a=gla_b_a[i], gla_norm=gla_norm[i])
        st = dict(cmp=cache_cmp_kv[i], slc=cache_slc_kv[i], win=state_win_kv[i], conv=state_conv[i], gla=state_gla[i],
                  page_table=page_table)
        branches, new = _mixers(i, z, tabs, wts, st)
        news.append(new)
        merged = merge(branches, (conv_w_out, sgu_w_out, nsa_w_out, gla_w_out), gates, i)
        delta = matmul(merged, w_out, tn=512, layer=i)
        x, h = add_norm(x, delta, ffn2_norm[i])
        delta = ffn_half(h, ffn2_w_in, ffn2_w_out, i)
        x, h = add_norm(x, delta, pe_norm[i])
        ple = jnp.concatenate([p_prompt[i].reshape(M_PROMPT, PLE_DIM), p_sample[i].reshape(DEC_BATCH, PLE_DIM),
                               jnp.zeros((n_tail, PLE_DIM), F32)], axis=0).astype(BF16)
        delta = ple_delta(h, ple, w_pe_gate, w_pe, i)
    _, y = add_norm(x, delta, final_norm, write_x=False, out_dtype=F32)
    stack = lambda name: jnp.stack([n[name] for n in news], axis=0)
    return (y[:M_PROMPT].reshape(BATCH, SEQ, D_MODEL), y[M_PROMPT:M_PROMPT + DEC_BATCH].reshape(DEC_BATCH, 1, D_MODEL),
            stack('cmp_p'), stack('cmp_s'), stack('slc_p'), stack('slc_s'), stack('win_p'), stack('win_s'),
            stack('conv_p'), stack('conv_s'), stack('gla_p'), stack('gla_s'), stack('sgu_v'))
```

```python
import functools
import math

import jax
import jax.numpy as jnp
from jax import lax
from jax.experimental import pallas as pl
from jax.experimental.pallas import tpu as pltpu

F32 = jnp.float32
BF16 = jnp.bfloat16

D_MODEL = 4096
BATCH = 4
SEQ = 2048
DEPTH = 2
DEC_BATCH = 8
PAST_LEN = 16384
PAGE_SIZE = 128
D_FF = 11008
W_MIX = 1024
N_BRANCH = 4
CONV_WIDTH = 31
SGU_CHUNK = 128
SGU_GROUPS = 8
HEAD_DIM = 128
NSA_HEADS = 8
KVH = 2
GRP = 4
CMP_STRIDE = 16
CMP_LEN = 32
SLC_LEN = 64
N_SELECT = 16
WINDOW = 512
Q_BLOCK = 128
FORCE_BONUS = 1.0e4
GLA_HEADS = 8
GLA_DV = 128
GLA_DK = 64
GLA_RANK = 16
GLA_TAU = 16.0
GLA_CHUNK = 64
REL_BUCKETS = 32
REL_MAX_DIST = 128
PLE_DIM = 256
EPS = 1e-6

LANES = 128
VMEM_LIMIT_BYTES = 56 * 1024 * 1024
M_PROMPT = BATCH * SEQ
M_SLAB = M_PROMPT + 16
TM = 912
TM_NORM = 304
FFN_CHUNK = 256
FFN_OUT_SLAB = 512
NEG = -1e30

C_VAL, C_GATE, C_U, C_V, C_Q, C_CMP, C_SLC, C_WIN, C_CG = 0, 1024, 2048, 3072, 4096, 5120, 5632, 6144, 6656
Z1W = 7168
Z2_OFF, Z2W = 6680, 2560
C2_DQ, C2_DK, C2_DV, C2_DA = 0, 512, 1024, 2048
Z3_OFF, Z3W = 8744, 1024
G_OFF = 9768

NT_DIMS = (((1,), (1,)), ((), ()))
TN_DIMS = (((0,), (0,)), ((), ()))


def _cp(n_axes, vmem=VMEM_LIMIT_BYTES):
    return pltpu.CompilerParams(dimension_semantics=("arbitrary",) * n_axes, vmem_limit_bytes=vmem)


def _dot(a, b):
    return jnp.dot(a, b, preferred_element_type=F32)


def _dot_nt(a, b):
    return lax.dot_general(a, b, NT_DIMS, preferred_element_type=F32)


def _dot_tn(a, b):
    return lax.dot_general(a, b, TN_DIMS, preferred_element_type=F32)


def _split_dot(x, w):
    hi = x.astype(BF16)
    lo = (x - hi.astype(F32)).astype(BF16)
    return _dot(hi, w) + _dot(lo, w)


def _silu(x):
    return x * jax.nn.sigmoid(x)


def _layernorm(x, g, b):
    mu = jnp.mean(x, axis=-1, keepdims=True)
    xc = x - mu
    var = jnp.mean(xc * xc, axis=-1, keepdims=True)
    return xc * lax.rsqrt(var + EPS) * g + b


def _masked_softmax(s, mask):
    s = jnp.where(mask, s, NEG)
    m = jnp.max(s, axis=-1, keepdims=True)
    p = jnp.where(mask, jnp.exp(s - m), 0.0)
    return p / jnp.maximum(jnp.sum(p, axis=-1, keepdims=True), 1e-30)


def _norm_kernel(*refs, has_delta, write_x):
    it = iter(refs)
    x_ref = next(it)
    d_ref = next(it) if has_delta else None
    g_ref = next(it)
    xo_ref = next(it) if write_x else None
    h_ref = next(it)
    x = x_ref[...]
    if has_delta:
        x = x + d_ref[...]
    if write_x:
        xo_ref[...] = x
    y = x * lax.rsqrt(jnp.mean(x * x, axis=-1, keepdims=True) + EPS) * g_ref[...]
    h_ref[...] = y.astype(h_ref.dtype)


def add_norm(x, delta, g, *, write_x=True, out_dtype=BF16):
    m, d = x.shape
    row = pl.BlockSpec((TM_NORM, d), lambda i: (i, 0))
    ins = [x] + ([delta] if delta is not None else []) + [g.reshape(1, d)]
    in_specs = [row] * (len(ins) - 1) + [pl.BlockSpec((1, d), lambda i: (0, 0))]
    out_shape = ([jax.ShapeDtypeStruct((m, d), F32)] if write_x else []) + [jax.ShapeDtypeStruct((m, d), out_dtype)]
    outs = pl.pallas_call(
        functools.partial(_norm_kernel, has_delta=delta is not None, write_x=write_x),
        grid=(m // TM_NORM,), in_specs=in_specs, out_specs=[row] * len(out_shape), out_shape=out_shape,
        compiler_params=_cp(1), name="add_norm")(*ins)
    return (outs[0], outs[1]) if write_x else (None, outs[0])


def final_norm_rows(x, delta, g, *, first_row, n_rows, tile):
    d = x.shape[1]
    first = first_row // tile
    row_in = pl.BlockSpec((tile, d), lambda i: (first + i, 0))
    return pl.pallas_call(
        functools.partial(_norm_kernel, has_delta=True, write_x=False), grid=(n_rows // tile,),
        in_specs=[row_in, row_in, pl.BlockSpec((1, d), lambda i: (0, 0))],
        out_specs=pl.BlockSpec((tile, d), lambda i: (i, 0)), out_shape=jax.ShapeDtypeStruct((n_rows, d), F32),
        compiler_params=_cp(1), name="final_norm")(x, delta, g.reshape(1, d))


def _ffn_kernel(h_ref, wg_ref, wu_ref, wo_ref, o_ref):
    @pl.when(pl.program_id(1) == 0)
    def _():
        o_ref[...] = jnp.zeros(o_ref.shape, F32)

    h = h_ref[...]
    g = _dot(h, wg_ref[...].astype(BF16))
    u = _dot(h, wu_ref[...].astype(BF16))
    a = (0.5 * _silu(g) * u).astype(BF16)
    for c0 in range(0, o_ref.shape[1], FFN_OUT_SLAB):
        cols = slice(c0, c0 + FFN_OUT_SLAB)
        o_ref[:, cols] += _dot(a, wo_ref[:, cols].astype(BF16))


def ffn_half(h, w_in, w_out, layer):
    m, d = h.shape
    nf = D_FF // FFN_CHUNK
    once = pl.Buffered(1)
    return pl.pallas_call(
        _ffn_kernel, grid=(m // TM, nf),
        in_specs=[pl.BlockSpec((TM, d), lambda i, f: (i, 0), pipeline_mode=once),
                  pl.BlockSpec((None, d, FFN_CHUNK), lambda i, f: (layer, 0, f)),
                  pl.BlockSpec((None, d, FFN_CHUNK), lambda i, f: (layer, 0, f + nf)),
                  pl.BlockSpec((None, FFN_CHUNK, d), lambda i, f: (layer, f, 0))],
        out_specs=pl.BlockSpec((TM, d), lambda i, f: (i, 0), pipeline_mode=once),
        out_shape=jax.ShapeDtypeStruct((m, d), F32), compiler_params=_cp(2), name="ffn")(h, w_in, w_in, w_out)


def _resident_bf16(w_ref, wb_ref):
    @pl.when(pl.program_id(1) == 0)
    def _():
        wb_ref[...] = w_ref[...].astype(BF16)

    return wb_ref[...]


def _mm_kernel(x_ref, w_ref, o_ref, *scratch, act):
    w = _resident_bf16(w_ref, scratch[0]) if scratch else w_ref[...]
    acc = _dot(x_ref[...], w)
    if act == "sigmoid":
        acc = jax.nn.sigmoid(acc)
    o_ref[...] = acc.astype(o_ref.dtype)


def _layer_block(shape, layer, index):
    return pl.BlockSpec((None,) + shape, lambda j, i: (layer,) + index(j, i))


def matmul(x, w, *, tn, layer=None, act=None, out_dtype=F32):
    m, k = x.shape
    n = w.shape[-1]
    if layer is None:
        w_spec, scratch = pl.BlockSpec((k, tn), lambda j, i: (0, j)), []
    else:
        w_spec, scratch = _layer_block((k, tn), layer, lambda j, i: (0, j)), [pltpu.VMEM((k, tn), BF16)]
    return pl.pallas_call(
        functools.partial(_mm_kernel, act=act), grid=(n // tn, m // TM),
        in_specs=[pl.BlockSpec((TM, k), lambda j, i: (i, 0)), w_spec],
        out_specs=pl.BlockSpec((TM, tn), lambda j, i: (i, j)), scratch_shapes=scratch,
        out_shape=jax.ShapeDtypeStruct((m, n), out_dtype), compiler_params=_cp(2), name="matmul")(x, w)


COL_TILE = 512


def _mm_window_kernel(x_ref, *refs, shift, act):
    w_refs, o_ref, wb_ref = refs[:-2], refs[-2], refs[-1]

    @pl.when(pl.program_id(1) == 0)
    def _():
        if shift == 0:
            wb_ref[...] = w_refs[0][...].astype(BF16)
        else:
            both = jnp.concatenate([w_refs[0][...], w_refs[1][...]], axis=1)
            wb_ref[...] = both[:, shift:shift + COL_TILE].astype(BF16)

    acc = _dot(x_ref[...], wb_ref[...])
    if act == "sigmoid":
        acc = jax.nn.sigmoid(acc)
    o_ref[...] = acc.astype(o_ref.dtype)


def matmul_window(x, w, layer, first_col, n_cols, *, act=None, out_dtype=F32):
    m, k = x.shape
    shift = first_col % LANES
    base = first_col - shift
    assert base % COL_TILE == 0 and n_cols % COL_TILE == 0
    w_specs = [_layer_block((k, COL_TILE), layer, lambda j, i: (0, base // COL_TILE + j))]
    if shift:
        per_tile = COL_TILE // LANES
        w_specs.append(_layer_block((k, LANES), layer, lambda j, i: (0, base // LANES + (j + 1) * per_tile)))
    return pl.pallas_call(
        functools.partial(_mm_window_kernel, shift=shift, act=act), grid=(n_cols // COL_TILE, m // TM),
        in_specs=[pl.BlockSpec((TM, k), lambda j, i: (i, 0))] + w_specs,
        out_specs=pl.BlockSpec((TM, COL_TILE), lambda j, i: (i, j)), scratch_shapes=[pltpu.VMEM((k, COL_TILE), BF16)],
        out_shape=jax.ShapeDtypeStruct((m, n_cols), out_dtype), compiler_params=_cp(2), name="matmul_window")(
            x, *([w] * len(w_specs)))


def _ple_kernel(h_ref, p_ref, wg_ref, wp_ref, o_ref, wgb_ref):
    gate = jax.nn.sigmoid(_dot(h_ref[...], _resident_bf16(wg_ref, wgb_ref)))
    o_ref[...] = gate * _dot(p_ref[...], wp_ref[...].astype(BF16))


def ple_delta(h, ple, w_gate, w_pe, layer, *, tn=512):
    m, k = h.shape
    n = w_gate.shape[-1]
    return pl.pallas_call(
        _ple_kernel, grid=(n // tn, m // TM),
        in_specs=[pl.BlockSpec((TM, k), lambda j, i: (i, 0)), pl.BlockSpec((TM, PLE_DIM), lambda j, i: (i, 0)),
                  _layer_block((k, tn), layer, lambda j, i: (0, j)),
                  _layer_block((PLE_DIM, tn), layer, lambda j, i: (0, j))],
        out_specs=pl.BlockSpec((TM, tn), lambda j, i: (i, j)), scratch_shapes=[pltpu.VMEM((k, tn), BF16)],
        out_shape=jax.ShapeDtypeStruct((m, n), F32), compiler_params=_cp(2), name="ple")(h, ple, w_gate, w_pe)


def _merge_kernel(*refs):
    o_refs, w_refs, g_refs, out_ref, wb_refs = refs[0:4], refs[4:8], refs[8:12], refs[12], refs[13:17]
    acc = None
    for o_ref, w_ref, g_ref, wb_ref in zip(o_refs, w_refs, g_refs, wb_refs):
        term = g_ref[...].astype(F32) * _dot(o_ref[...], _resident_bf16(w_ref, wb_ref))
        acc = term if acc is None else acc + term
    out_ref[...] = acc.astype(out_ref.dtype)


def merge(branches, weights, gates, layer, *, tn=512):
    m = branches[0].shape[0]
    nb = D_MODEL // tn
    in_specs = ([pl.BlockSpec((TM, W_MIX), lambda j, i: (i, 0))] * 4
                + [_layer_block((W_MIX, tn), layer, lambda j, i: (0, j))] * 4
                + [pl.BlockSpec((TM, tn), lambda j, i, b=b: (i, b * nb + j)) for b in range(4)])
    return pl.pallas_call(
        _merge_kernel, grid=(nb, m // TM), in_specs=in_specs,
        out_specs=pl.BlockSpec((TM, tn), lambda j, i: (i, j)), scratch_shapes=[pltpu.VMEM((W_MIX, tn), BF16)] * 4,
        out_shape=jax.ShapeDtypeStruct((m, D_MODEL), BF16), compiler_params=_cp(2), name="merge")(
            *branches, *weights, gates, gates, gates, gates)


CONV_HALO = 32


def _conv_kernel(val_ref, gate_ref, halo_ref, w_ref, cb_ref, lg_ref, lb_ref, o_ref, st_ref, aext, *, tt, rb, n_last):
    t = pl.program_id(1)

    @pl.when(t == 0)
    def _():
        aext[0:CONV_HALO, :] = halo_ref[...]

    aext[CONV_HALO:CONV_HALO + tt, :] = val_ref[...] * jax.nn.sigmoid(gate_ref[...])
    first = CONV_HALO - (CONV_WIDTH - 1)
    for r0 in range(0, tt, rb):
        acc = jnp.zeros((rb, W_MIX), F32) + cb_ref[...]
        for j in range(CONV_WIDTH):
            acc = acc + aext[r0 + first + j:r0 + first + j + rb, :] * w_ref[j:j + 1, :]
        y = _silu(_layernorm(acc, lg_ref[...], lb_ref[...]))
        o_ref[r0:r0 + rb, :] = y.astype(o_ref.dtype)

    @pl.when(t == pl.num_programs(1) - 1)
    def _():
        st_ref[...] = aext[n_last:n_last + CONV_HALO, :]

    if tt >= CONV_HALO:
        aext[0:CONV_HALO, :] = aext[tt:tt + CONV_HALO, :]


def conv_mixer(z, halo, conv_w, conv_b, ln_g, ln_b, *, nb, t_len, tt, rb, n_last):
    nt = t_len // tt
    w = jnp.zeros((CONV_HALO, W_MIX), F32).at[:CONV_WIDTH].set(conv_w)
    vec = pl.BlockSpec((1, W_MIX), lambda b, t: (0, 0))
    return pl.pallas_call(
        functools.partial(_conv_kernel, tt=tt, rb=rb, n_last=n_last), grid=(nb, nt),
        in_specs=[pl.BlockSpec((tt, W_MIX), lambda b, t: (b * nt + t, C_VAL // W_MIX)),
                  pl.BlockSpec((tt, W_MIX), lambda b, t: (b * nt + t, C_GATE // W_MIX)),
                  pl.BlockSpec((None, CONV_HALO, W_MIX), lambda b, t: (b, 0, 0)),
                  pl.BlockSpec((CONV_HALO, W_MIX), lambda b, t: (0, 0)), vec, vec, vec],
        out_specs=[pl.BlockSpec((tt, W_MIX), lambda b, t: (b * nt + t, 0)),
                   pl.BlockSpec((None, CONV_HALO, W_MIX), lambda b, t: (b, 0, 0))],
        out_shape=[jax.ShapeDtypeStruct((nb * t_len, W_MIX), BF16), jax.ShapeDtypeStruct((nb, CONV_HALO, W_MIX), F32)],
        scratch_shapes=[pltpu.VMEM((CONV_HALO + tt, W_MIX), F32)],
        compiler_params=_cp(2), name="conv")(z, z, halo, w, conv_b.reshape(1, -1), ln_g.reshape(1, -1), ln_b.reshape(1, -1))


def _sgu_kernel(u_ref, v_ref, wc_ref, bs_ref, lg_ref, lb_ref, o_ref, *, tt):
    for c0 in range(0, tt, SGU_CHUNK):
        rows = slice(c0, c0 + SGU_CHUNK)
        u = jax.nn.gelu(u_ref[rows, :])
        v = _layernorm(jax.nn.gelu(v_ref[rows, :]), lg_ref[...], lb_ref[...]).astype(BF16)
        parts = [_dot(wc_ref[g], v[:, g * LANES:(g + 1) * LANES]) for g in range(SGU_GROUPS)]
        mix = jnp.concatenate(parts, axis=1) + bs_ref[...]
        o_ref[rows, :] = (u * mix).astype(o_ref.dtype)


def sgu_mixer(z, w_causal, bias_rows, ln_g, ln_b, *, rows, tt=256):
    vec = pl.BlockSpec((1, W_MIX), lambda i: (0, 0))
    return pl.pallas_call(
        functools.partial(_sgu_kernel, tt=tt), grid=(rows // tt,),
        in_specs=[pl.BlockSpec((tt, W_MIX), lambda i: (i, C_U // W_MIX)),
                  pl.BlockSpec((tt, W_MIX), lambda i: (i, C_V // W_MIX)),
                  pl.BlockSpec((SGU_GROUPS, SGU_CHUNK, SGU_CHUNK), lambda i: (0, 0, 0)),
                  pl.BlockSpec((SGU_CHUNK, W_MIX), lambda i: (0, 0)), vec, vec],
        out_specs=pl.BlockSpec((tt, W_MIX), lambda i: (i, 0)),
        out_shape=jax.ShapeDtypeStruct((rows, W_MIX), BF16), compiler_params=_cp(1), name="sgu")(
            z, z, w_causal, bias_rows, ln_g.reshape(1, -1), ln_b.reshape(1, -1))


def _sgu_step_kernel(u_ref, v_ref, w0_ref, b0_ref, lg_ref, lb_ref, o_ref, vo_ref):
    u = jax.nn.gelu(u_ref[...])
    v = _layernorm(jax.nn.gelu(v_ref[...]), lg_ref[...], lb_ref[...])
    vo_ref[...] = v
    o_ref[...] = (u * (v * w0_ref[...] + b0_ref[...])).astype(o_ref.dtype)


def sgu_step(zs, w0, b0, ln_g, ln_b):
    n = zs.shape[0]
    vec = pl.BlockSpec((1, W_MIX), lambda i: (0, 0))
    return pl.pallas_call(
        _sgu_step_kernel, grid=(1,),
        in_specs=[pl.BlockSpec((n, W_MIX), lambda i: (0, C_U // W_MIX)),
                  pl.BlockSpec((n, W_MIX), lambda i: (0, C_V // W_MIX)), vec, vec, vec, vec],
        out_specs=[pl.BlockSpec((n, W_MIX), lambda i: (0, 0))] * 2,
        out_shape=[jax.ShapeDtypeStruct((n, W_MIX), BF16), jax.ShapeDtypeStruct((n, W_MIX), F32)],
        compiler_params=_cp(1), name="sgu_step")(zs, zs, w0, b0, ln_g.reshape(1, -1), ln_b.reshape(1, -1))


def _log_sigmoid(x):
    return jnp.minimum(x, 0.0) - jnp.log1p(jnp.exp(-jnp.abs(x)))


GLA_PAIRS = GLA_HEADS // 2
GLA_KEYS = GLA_HEADS * GLA_DK


def _gla_kernel(q_ref, k_ref, v_ref, r_ref, da_ref, wa_ref, ba_ref, gn_ref, s0_ref, o_ref, so_ref, st, *, tt, chunk, n_valid):
    t = pl.program_id(1)

    @pl.when(t == 0)
    def _():
        st[...] = s0_ref[...]

    row = lax.broadcasted_iota(jnp.int32, (chunk, chunk), 0)
    col = lax.broadcasted_iota(jnp.int32, (chunk, chunk), 1)
    causal = row >= col
    tril = causal.astype(BF16)
    low_half = lax.broadcasted_iota(jnp.int32, (chunk, LANES), 1) < GLA_DK
    for c0 in range(0, tt, chunk):
        rows = slice(c0, c0 + chunk)
        la = _log_sigmoid(_dot(da_ref[rows, :].astype(BF16), wa_ref[...]) + ba_ref[...]) * (1.0 / GLA_TAU)
        if n_valid < chunk:
            la = jnp.where(lax.broadcasted_iota(jnp.int32, la.shape, 0) < n_valid, la, 0.0)
        b = _split_dot_left(tril, la)
        bl = b[chunk - 1:chunk, :]
        kk = k_ref[rows, :]
        q_in = q_ref[rows, :] * (GLA_DK ** -0.5) * jnp.exp(b)
        k_in = (kk * jnp.exp(-b)).astype(BF16)
        k_end = kk * jnp.exp(bl - b)
        decay = jnp.exp(bl)
        vv = v_ref[rows, :].astype(BF16)
        for p in range(GLA_PAIRS):
            ps = slice(p * LANES, (p + 1) * LANES)
            state = st[p]
            state_b = state.astype(BF16)
            update = state * decay[:, ps]
            for e in range(2):
                hs = slice((2 * p + e) * GLA_DV, (2 * p + e + 1) * GLA_DV)
                mine = low_half if e == 0 else jnp.logical_not(low_half)
                q_h = jnp.where(mine, q_in[:, ps], 0.0).astype(BF16)
                k_h = jnp.where(mine, k_end[:, ps], 0.0).astype(BF16)
                a = jnp.where(causal, _dot_nt(q_h, k_in[:, ps]), 0.0).astype(BF16)
                o = _dot(a, vv[:, hs]) + _dot_nt(q_h, state_b)
                update = update + _dot_tn(vv[:, hs], k_h)
                y = o * lax.rsqrt(jnp.mean(o * o, axis=-1, keepdims=True) + EPS) * gn_ref[...]
                o_ref[rows, hs] = (y * _silu(r_ref[rows, hs])).astype(o_ref.dtype)
            st[p] = update

    @pl.when(t == pl.num_programs(1) - 1)
    def _():
        so_ref[...] = st[...]


def _split_dot_left(w, x):
    hi = x.astype(BF16)
    lo = (x - hi.astype(F32)).astype(BF16)
    return _dot(w, hi) + _dot(w, lo)


def gla_state_in(s):
    nb = s.shape[0]
    s = s.reshape(nb, GLA_PAIRS, 2, GLA_DK, GLA_DV)
    return jnp.transpose(s, (0, 1, 4, 2, 3)).reshape(nb, GLA_PAIRS, GLA_DV, 2 * GLA_DK)


def gla_state_out(s):
    nb = s.shape[0]
    s = s.reshape(nb, GLA_PAIRS, GLA_DV, 2, GLA_DK)
    return jnp.transpose(s, (0, 1, 3, 4, 2)).reshape(nb, GLA_HEADS, GLA_DK, GLA_DV)


def gla_mixer(z2, z3, s0_t, w_a2p, b_a, g_norm, *, nb, t_len, tt, chunk, n_valid):
    nt = t_len // tt
    rows = lambda width, c: pl.BlockSpec((tt, width), lambda b, t: (b * nt + t, c // width))
    state = pl.BlockSpec((None, GLA_PAIRS, GLA_DV, LANES), lambda b, t: (b, 0, 0, 0))
    return pl.pallas_call(
        functools.partial(_gla_kernel, tt=tt, chunk=chunk, n_valid=n_valid), grid=(nb, nt),
        in_specs=[rows(GLA_KEYS, C2_DQ), rows(GLA_KEYS, C2_DK), rows(W_MIX, C2_DV), rows(W_MIX, 0), rows(LANES, C2_DA),
                  pl.BlockSpec((LANES, GLA_KEYS), lambda b, t: (0, 0)),
                  pl.BlockSpec((1, GLA_KEYS), lambda b, t: (0, 0)),
                  pl.BlockSpec((1, LANES), lambda b, t: (0, 0)), state],
        out_specs=[pl.BlockSpec((tt, W_MIX), lambda b, t: (b * nt + t, 0)), state],
        out_shape=[jax.ShapeDtypeStruct((nb * t_len, W_MIX), BF16),
                   jax.ShapeDtypeStruct((nb, GLA_PAIRS, GLA_DV, LANES), F32)],
        scratch_shapes=[pltpu.VMEM((GLA_PAIRS, GLA_DV, LANES), F32)],
        compiler_params=_cp(2), name="gla")(z2, z2, z2, z3, z2, w_a2p, b_a.reshape(1, -1), g_norm.reshape(1, -1), s0_t)


SEG_PER_STEP = 128
CMP_ROWS = SEG_PER_STEP * CMP_STRIDE


def _compress_kernel(*refs, n_in, tail_from_next):
    pos = 1 if tail_from_next else 0
    in_refs = refs[pos:pos + n_in]
    w_ref, o_ref = refs[pos + n_in], refs[pos + n_in + 1]
    xs = refs[pos + n_in + 2:]
    n_main = n_in - 1 if tail_from_next else n_in
    rows_each = CMP_ROWS // n_main
    for ch in range(4):
        cols = slice(ch * LANES, (ch + 1) * LANES)
        for j in range(n_main):
            xs[ch][j * rows_each:(j + 1) * rows_each, :] = in_refs[j][:, cols]
        if tail_from_next:
            xs[ch][CMP_ROWS:CMP_ROWS + CMP_STRIDE, :] = in_refs[n_in - 1][0:CMP_STRIDE, cols]
        else:
            xs[ch][CMP_ROWS:CMP_ROWS + CMP_STRIDE, :] = jnp.zeros((CMP_STRIDE, LANES), F32)
        acc = jnp.zeros((SEG_PER_STEP, LANES), F32)
        for r in range(CMP_LEN):
            x = xs[ch][pl.ds(r, SEG_PER_STEP, stride=CMP_STRIDE), :]
            acc = acc + _dot(x.astype(BF16), w_ref[ch // 2, r])
        o_ref[:, cols] = acc


def compress_prompt(z, w_cmp):
    return pl.pallas_call(
        functools.partial(_compress_kernel, n_in=1, tail_from_next=False), grid=(BATCH,),
        in_specs=[pl.BlockSpec((CMP_ROWS, 4 * LANES), lambda b: (b, C_CMP // (4 * LANES))),
                  pl.BlockSpec((2, CMP_LEN, LANES, LANES), lambda b: (0, 0, 0, 0))],
        out_specs=pl.BlockSpec((None, SEG_PER_STEP, 4 * LANES), lambda b: (b, 0, 0)),
        out_shape=jax.ShapeDtypeStruct((BATCH, SEG_PER_STEP, 4 * LANES), F32),
        scratch_shapes=[pltpu.VMEM((CMP_ROWS + CMP_STRIDE, LANES), F32)] * 4,
        compiler_params=_cp(1), name="compress_prompt")(z, w_cmp)


PAGES_PER_STEP = CMP_ROWS // PAGE_SIZE
N_PAGES = PAST_LEN // PAGE_SIZE


def compress_paged(cache, page_table, w_cmp):
    n_steps = N_PAGES // PAGES_PER_STEP

    def page_spec(j):
        return pl.BlockSpec((None, PAGE_SIZE, 4 * LANES),
                            lambda b, c, pt: (pt[b, jnp.minimum(c * PAGES_PER_STEP + j, N_PAGES - 1)], 0, 0))

    grid_spec = pltpu.PrefetchScalarGridSpec(
        num_scalar_prefetch=1, grid=(DEC_BATCH, n_steps),
        in_specs=[page_spec(j) for j in range(PAGES_PER_STEP + 1)]
        + [pl.BlockSpec((2, CMP_LEN, LANES, LANES), lambda b, c, pt: (0, 0, 0, 0))],
        out_specs=pl.BlockSpec((None, SEG_PER_STEP, 4 * LANES), lambda b, c, pt: (b, c, 0)),
        scratch_shapes=[pltpu.VMEM((CMP_ROWS + CMP_STRIDE, LANES), F32)] * 4)
    return pl.pallas_call(
        functools.partial(_compress_kernel, n_in=PAGES_PER_STEP + 1, tail_from_next=True), grid_spec=grid_spec,
        out_shape=jax.ShapeDtypeStruct((DEC_BATCH, n_steps * SEG_PER_STEP, 4 * LANES), F32),
        compiler_params=_cp(2), name="compress_paged")(page_table, *([cache] * (PAGES_PER_STEP + 1)), w_cmp)


def _top_select(score, n_lanes):
    lane = lax.broadcasted_iota(jnp.int32, score.shape, 1).astype(F32)
    sel = jnp.zeros(score.shape, F32)
    for _ in range(N_SELECT):
        m = jnp.max(score, axis=-1, keepdims=True)
        idx = jnp.min(jnp.where(score == m, lane, float(n_lanes)), axis=-1, keepdims=True)
        pick = lane == idx
        sel = jnp.where(pick, 1.0, sel)
        score = jnp.where(pick, -3e38, score)
    return sel


def _bias_tiles(delta_of_tile, n_tiles, t0, t1, t2):
    tiles = []
    for j in range(n_tiles):
        delta = delta_of_tile(j)
        tiles.append(jnp.where(delta == 0, t0, jnp.where(delta == 1, t1, t2)))
    return jnp.concatenate(tiles, axis=1)


WIN_BAND = WINDOW + Q_BLOCK


SLC_KEY_STEP = 512


def _nsa_prompt_kernel(q_ref, cbk_ref, cbv_ref, ks_ref, vs_ref, kw_ref, vw_ref, g_ref, bc_ref, t0_ref, t1_ref, t2_ref,
                       covt_ref, e_ref, o_ref, oslc_ref):
    h = pl.program_id(1)
    i = pl.program_id(2)
    q = q_ref[...] * (HEAD_DIM ** -0.5)
    qs = jnp.concatenate([q[:, g * LANES:(g + 1) * LANES] for g in range(GRP)], axis=0).astype(BF16)
    qpos = i * Q_BLOCK + lax.broadcasted_iota(jnp.int32, (Q_BLOCK, 1), 0)
    lane = lax.broadcasted_iota(jnp.int32, (Q_BLOCK, LANES), 1)

    sc = _dot_nt(qs, cbk_ref[...].astype(BF16))
    cmask = (lane * CMP_STRIDE + (CMP_LEN - 1) <= qpos) & (lane < SEQ // CMP_STRIDE - 1)
    p_c = [_masked_softmax(sc[g * Q_BLOCK:(g + 1) * Q_BLOCK] + bc_ref[g], cmask) for g in range(GRP)]
    o_cmp = _dot(jnp.concatenate(p_c, axis=0).astype(BF16), cbv_ref[...].astype(BF16))
    psum = p_c[0] + p_c[1] + p_c[2] + p_c[3]
    p_hi = psum.astype(BF16)
    p_lo = (psum - p_hi.astype(F32)).astype(BF16)
    imp_t = _dot_nt(covt_ref[...], p_hi) + _dot_nt(covt_ref[...], p_lo)

    n_blk = SEQ // SLC_LEN
    blk = lax.broadcasted_iota(jnp.int32, (n_blk, Q_BLOCK), 0)
    qpos_l = i * Q_BLOCK + lax.broadcasted_iota(jnp.int32, (n_blk, Q_BLOCK), 1)
    cur = lax.shift_right_logical(qpos_l, 6)
    valid = blk * SLC_LEN <= qpos_l
    forced = (blk == 0) | (blk == cur) | (blk == cur - 1)
    score = jnp.where(valid, imp_t[0:n_blk] + jnp.where(forced, FORCE_BONUS, 0.0), NEG)
    rank = jnp.zeros((n_blk, Q_BLOCK), F32)
    for other_blk in range(n_blk):
        other = score[other_blk:other_blk + 1, :]
        rank = rank + jnp.where((other > score) | ((other == score) & (blk > other_blk)), 1.0, 0.0)
    sel_t = jnp.where(valid & (rank < N_SELECT), 1.0, 0.0).astype(BF16)

    def selected(n_keys):
        kpos = lax.broadcasted_iota(jnp.int32, (Q_BLOCK, n_keys), 1)
        smask = (_dot_tn(sel_t, e_ref[0:n_blk, 0:n_keys]) > 0.5) & (kpos <= qpos)
        ss = _dot_nt(qs, ks_ref[0:n_keys, :].astype(BF16))
        p_s = []
        for g in range(GRP):
            bias = _bias_tiles(lambda j: i - j, n_keys // Q_BLOCK, t0_ref[g], t1_ref[g], t2_ref[g])
            p_s.append(_masked_softmax(ss[g * Q_BLOCK:(g + 1) * Q_BLOCK] + bias, smask))
        oslc_ref[...] = _dot(jnp.concatenate(p_s, axis=0).astype(BF16), vs_ref[0:n_keys, :].astype(BF16))

    tiles_per_step = SLC_KEY_STEP // Q_BLOCK
    for hi in range(tiles_per_step, SEQ // Q_BLOCK + 1, tiles_per_step):
        pl.when((i >= hi - tiles_per_step) & (i < hi))(functools.partial(selected, hi * Q_BLOCK))
    o_slc = oslc_ref[...]

    first_tile = jnp.maximum(i - WINDOW // Q_BLOCK, 0)
    start = pl.multiple_of(first_tile * Q_BLOCK, Q_BLOCK)
    dist = qpos - (start + lax.broadcasted_iota(jnp.int32, (Q_BLOCK, WIN_BAND), 1))
    wmask = (dist >= 0) & (dist <= WINDOW)
    sw = _dot_nt(qs, kw_ref[pl.ds(start, WIN_BAND), :].astype(BF16))
    p_w = []
    for g in range(GRP):
        bias = _bias_tiles(lambda j: i - first_tile - j, WIN_BAND // Q_BLOCK, t0_ref[g], t1_ref[g], t2_ref[g])
        p_w.append(_masked_softmax(sw[g * Q_BLOCK:(g + 1) * Q_BLOCK] + bias, wmask))
    o_win = _dot(jnp.concatenate(p_w, axis=0).astype(BF16), vw_ref[pl.ds(start, WIN_BAND), :].astype(BF16))

    gates = jax.nn.sigmoid(g_ref[...])
    for g in range(GRP):
        rows = slice(g * Q_BLOCK, (g + 1) * Q_BLOCK)
        base = (h * GRP + g) * 3
        gc = [jnp.sum(jnp.where(lane == base + br, gates, 0.0), axis=-1, keepdims=True) for br in range(3)]
        o = gc[0] * o_cmp[rows] + gc[1] * o_slc[rows] + gc[2] * o_win[rows]
        o_ref[:, g * LANES:(g + 1) * LANES] = o.astype(o_ref.dtype)


def nsa_prompt(z, cb, bias_c, t0, t1, t2, cover_t, expand):
    nq = SEQ // Q_BLOCK
    kv = lambda c: pl.BlockSpec((SEQ, LANES), lambda b, h, i, c=c: (b, c // LANES + h))
    cbs = lambda c: pl.BlockSpec((None, SEG_PER_STEP, LANES), lambda b, h, i, c=c: (b, 0, c + h))
    toe = pl.BlockSpec((GRP, Q_BLOCK, LANES), lambda b, h, i: (h, 0, 0))
    return pl.pallas_call(
        _nsa_prompt_kernel, grid=(BATCH, KVH, nq),
        in_specs=[pl.BlockSpec((Q_BLOCK, GRP * LANES), lambda b, h, i: (b * nq + i, C_Q // (GRP * LANES) + h)),
                  cbs(0), cbs(KVH), kv(C_SLC), kv(C_SLC + KVH * LANES), kv(C_WIN), kv(C_WIN + KVH * LANES),
                  pl.BlockSpec((Q_BLOCK, LANES), lambda b, h, i: (b * nq + i, C_CG // LANES)),
                  pl.BlockSpec((GRP, Q_BLOCK, LANES), lambda b, h, i: (h, i, 0)), toe, toe, toe,
                  pl.BlockSpec((LANES, LANES), lambda b, h, i: (0, 0)),
                  pl.BlockSpec((LANES, SEQ), lambda b, h, i: (0, 0))],
        out_specs=pl.BlockSpec((Q_BLOCK, GRP * LANES), lambda b, h, i: (b * nq + i, h)),
        scratch_shapes=[pltpu.VMEM((GRP * Q_BLOCK, LANES), F32)],
        out_shape=jax.ShapeDtypeStruct((M_PROMPT, W_MIX), BF16), compiler_params=_cp(3), name="nsa_prompt")(
            z, cb, cb, z, z, z, z, z, bias_c, t0, t1, t2, cover_t, expand)


N_CMP_S = PAST_LEN // CMP_STRIDE
N_SLC_S = 384


def _head_rows(shape):
    return lax.broadcasted_iota(jnp.int32, shape, 0) < GRP


def _nsa_step_a_kernel(q_ref, cb_ref, bc_ref, cov_ref, wb_ref, kvn_ref, bw_ref, bwn_ref, ocw_ref, sel_ref):
    q = q_ref[...] * (HEAD_DIM ** -0.5)
    qb = q.astype(BF16)
    cb = cb_ref[...]
    first = _head_rows((NSA_HEADS, N_CMP_S))
    s = jnp.where(first, _dot_nt(qb, cb[:, 0:LANES].astype(BF16)), _dot_nt(qb, cb[:, LANES:2 * LANES].astype(BF16)))
    kidx = lax.broadcasted_iota(jnp.int32, (NSA_HEADS, N_CMP_S), 1)
    p = _masked_softmax(s + bc_ref[...], kidx * CMP_STRIDE + (CMP_LEN - 1) <= PAST_LEN)
    pb = p.astype(BF16)
    o_cmp = jnp.where(_head_rows((NSA_HEADS, LANES)), _dot(pb, cb[:, 2 * LANES:3 * LANES].astype(BF16)),
                      _dot(pb, cb[:, 3 * LANES:4 * LANES].astype(BF16)))
    row = lax.broadcasted_iota(jnp.int32, (NSA_HEADS, N_CMP_S), 0)
    ps0 = jnp.sum(jnp.where(first, p, 0.0), axis=0, keepdims=True)
    ps1 = jnp.sum(jnp.where(first, 0.0, p), axis=0, keepdims=True)
    psum = jnp.where(row == 0, ps0, jnp.where(row == 1, ps1, 0.0))
    imp = _split_dot(psum, cov_ref[...])
    lane = lax.broadcasted_iota(jnp.int32, (NSA_HEADS, N_SLC_S), 1)
    cur = PAST_LEN // SLC_LEN
    valid = lane <= cur
    forced = (lane == 0) | (lane == cur) | (lane == cur - 1)
    score = jnp.where(valid, imp + jnp.where(forced, FORCE_BONUS, 0.0), NEG)
    sel_ref[...] = _top_select(score, N_SLC_S) * valid.astype(F32)

    wb = wb_ref[...]
    firstw = _head_rows((NSA_HEADS, WINDOW))
    sw = jnp.where(firstw, _dot_nt(qb, wb[:, 0:LANES].astype(BF16)), _dot_nt(qb, wb[:, LANES:2 * LANES].astype(BF16)))
    sw = sw + bw_ref[...]
    firstl = _head_rows((NSA_HEADS, LANES))
    kvn = kvn_ref[...]
    k_new = jnp.where(firstl, kvn[:, 0:LANES], kvn[:, LANES:2 * LANES])
    v_new = jnp.where(firstl, kvn[:, 2 * LANES:3 * LANES], kvn[:, 3 * LANES:4 * LANES])
    s_new = jnp.sum(q * k_new, axis=-1, keepdims=True) + bwn_ref[:, 0:1]
    m = jnp.maximum(jnp.max(sw, axis=-1, keepdims=True), s_new)
    pw = jnp.exp(sw - m)
    pn = jnp.exp(s_new - m)
    pwb = pw.astype(BF16)
    acc = jnp.where(firstl, _dot(pwb, wb[:, 2 * LANES:3 * LANES].astype(BF16)), _dot(pwb, wb[:, 3 * LANES:4 * LANES].astype(BF16)))
    o_win = (acc + pn * v_new) / (jnp.sum(pw, axis=-1, keepdims=True) + pn)
    ocw_ref[:, 0:LANES] = o_cmp
    ocw_ref[:, LANES:2 * LANES] = o_win


def nsa_step_a(q_s, cb_s, bias_c, cover, win_buf, kvw_new, bias_w, bias_wn):
    per_b = lambda *shape: pl.BlockSpec((None,) + shape, lambda b: (b,) + (0,) * len(shape))
    whole = lambda *shape: pl.BlockSpec(shape, lambda b: (0,) * len(shape))
    return pl.pallas_call(
        _nsa_step_a_kernel, grid=(DEC_BATCH,),
        in_specs=[per_b(NSA_HEADS, LANES), per_b(N_CMP_S, 4 * LANES), whole(NSA_HEADS, N_CMP_S), whole(N_CMP_S, N_SLC_S),
                  per_b(WINDOW, 4 * LANES), per_b(1, 4 * LANES), whole(NSA_HEADS, WINDOW), whole(NSA_HEADS, LANES)],
        out_specs=[per_b(NSA_HEADS, 2 * LANES), per_b(NSA_HEADS, N_SLC_S)],
        out_shape=[jax.ShapeDtypeStruct((DEC_BATCH, NSA_HEADS, 2 * LANES), F32),
                   jax.ShapeDtypeStruct((DEC_BATCH, NSA_HEADS, N_SLC_S), F32)],
        compiler_params=_cp(1), name="nsa_step_a")(q_s, cb_s, bias_c, cover, win_buf, kvw_new, bias_w, bias_wn)


def _nsa_step_b_kernel(*refs):
    pages = refs[1:1 + PAGES_PER_STEP]
    (q_ref, sel_ref, e_ref, bs_ref, kvn_ref, seln_ref, bn_ref, ocw_ref, g_ref, o_ref, m_sc, l_sc, acc_sc) = refs[1 + PAGES_PER_STEP:]
    c = pl.program_id(1)

    @pl.when(c == 0)
    def _():
        m_sc[...] = jnp.full(m_sc.shape, NEG, F32)
        l_sc[...] = jnp.zeros(l_sc.shape, F32)
        acc_sc[...] = jnp.zeros(acc_sc.shape, F32)

    q = q_ref[...] * (HEAD_DIM ** -0.5)
    qb = q.astype(BF16)
    kv = [jnp.concatenate([pg[:, j * LANES:(j + 1) * LANES].astype(BF16) for pg in pages], axis=0) for j in range(4)]
    first = _head_rows((NSA_HEADS, CMP_ROWS))
    firstl = _head_rows((NSA_HEADS, LANES))
    s = jnp.where(first, _dot_nt(qb, kv[0]), _dot_nt(qb, kv[1])) + bs_ref[...]
    mask = _dot(sel_ref[...].astype(BF16), e_ref[...]) > 0.5
    s = jnp.where(mask, s, NEG)
    m_old = m_sc[...]
    m_new = jnp.maximum(m_old, jnp.max(s, axis=-1, keepdims=True))
    alpha = jnp.exp(m_old - m_new)
    p = jnp.where(mask, jnp.exp(s - m_new), 0.0)
    pb = p.astype(BF16)
    l_sc[...] = alpha * l_sc[...] + jnp.sum(p, axis=-1, keepdims=True)
    acc_sc[...] = alpha * acc_sc[...] + jnp.where(firstl, _dot(pb, kv[2]), _dot(pb, kv[3]))
    m_sc[...] = m_new

    @pl.when(c == pl.num_programs(1) - 1)
    def _():
        kvn = kvn_ref[...]
        k_new = jnp.where(firstl, kvn[:, 0:LANES], kvn[:, LANES:2 * LANES])
        v_new = jnp.where(firstl, kvn[:, 2 * LANES:3 * LANES], kvn[:, 3 * LANES:4 * LANES])
        on = seln_ref[:, 0:1] > 0.5
        s_n = jnp.where(on, jnp.sum(q * k_new, axis=-1, keepdims=True) + bn_ref[:, 0:1], NEG)
        m_old = m_sc[...]
        m_fin = jnp.maximum(m_old, s_n)
        alpha = jnp.exp(m_old - m_fin)
        p_n = jnp.where(on, jnp.exp(s_n - m_fin), 0.0)
        l_fin = alpha * l_sc[...] + p_n
        o_slc = (alpha * acc_sc[...] + p_n * v_new) / jnp.maximum(l_fin, 1e-30)
        gates = jax.nn.sigmoid(g_ref[...])
        o = gates[:, 0:1] * ocw_ref[:, 0:LANES] + gates[:, 1:2] * o_slc + gates[:, 2:3] * ocw_ref[:, LANES:2 * LANES]
        o_ref[...] = o.astype(o_ref.dtype)


def nsa_step_b(cache, page_table, q_s, sel_r, expand, bias_s, kvs_new, sel_new, bias_n, ocw, gates):
    n_steps = N_PAGES // PAGES_PER_STEP

    def page_spec(j):
        return pl.BlockSpec((None, PAGE_SIZE, 4 * LANES), lambda b, c, pt: (pt[b, c * PAGES_PER_STEP + j], 0, 0))

    per_b = lambda *shape: pl.BlockSpec((None,) + shape, lambda b, c, pt: (b,) + (0,) * len(shape))
    whole = lambda *shape: pl.BlockSpec(shape, lambda b, c, pt: (0,) * len(shape))
    grid_spec = pltpu.PrefetchScalarGridSpec(
        num_scalar_prefetch=1, grid=(DEC_BATCH, n_steps),
        in_specs=[page_spec(j) for j in range(PAGES_PER_STEP)] + [
            per_b(NSA_HEADS, LANES),
            pl.BlockSpec((None, None, NSA_HEADS, LANES), lambda b, c, pt: (b, c, 0, 0)),
            whole(LANES, CMP_ROWS),
            pl.BlockSpec((NSA_HEADS, CMP_ROWS), lambda b, c, pt: (0, c)),
            per_b(1, 4 * LANES), per_b(NSA_HEADS, LANES), whole(NSA_HEADS, LANES),
            per_b(NSA_HEADS, 2 * LANES), per_b(NSA_HEADS, LANES)],
        out_specs=per_b(NSA_HEADS, LANES),
        scratch_shapes=[pltpu.VMEM((NSA_HEADS, 1), F32), pltpu.VMEM((NSA_HEADS, 1), F32), pltpu.VMEM((NSA_HEADS, LANES), F32)])
    return pl.pallas_call(
        _nsa_step_b_kernel, grid_spec=grid_spec,
        out_shape=jax.ShapeDtypeStruct((DEC_BATCH, NSA_HEADS, LANES), BF16),
        compiler_params=_cp(2), name="nsa_step_b")(
            page_table, *([cache] * PAGES_PER_STEP), q_s, sel_r, expand, bias_s, kvs_new, sel_new, bias_n, ocw, gates)


def _rel_bias_of(rel_bias, dist):
    n = jnp.maximum(dist, 0)
    max_exact = REL_BUCKETS // 2
    nf = jnp.maximum(n, 1).astype(F32)
    large = max_exact + (jnp.log(nf / max_exact) / math.log(REL_MAX_DIST / max_exact)
                         * (REL_BUCKETS - max_exact)).astype(jnp.int32)
    bucket = jnp.where(n < max_exact, n, jnp.minimum(large, REL_BUCKETS - 1))[None]
    out = jnp.zeros((NSA_HEADS,) + dist.shape, F32)
    for k in range(REL_BUCKETS):
        out = jnp.where(bucket == k, rel_bias[k].reshape((NSA_HEADS,) + (1,) * dist.ndim), out)
    return out


def _nsa_tables(rel_bias):
    bias_of = functools.partial(_rel_bias_of, rel_bias)
    r = jnp.arange(Q_BLOCK)
    diff = r[:, None] - r[None, :]
    t0 = bias_of(diff)
    t1 = bias_of(Q_BLOCK + diff)
    t2 = bias_of(2 * Q_BLOCK + diff)
    bias_c = bias_of(jnp.arange(SEQ)[:, None] - (jnp.arange(LANES)[None, :] * CMP_STRIDE + CMP_LEN - 1))
    k = jnp.arange(LANES)
    j = jnp.arange(LANES)
    cover = ((k[:, None] * CMP_STRIDE < (j[None, :] + 1) * SLC_LEN) & (k[:, None] * CMP_STRIDE + CMP_LEN > j[None, :] * SLC_LEN)
             & (k[:, None] < SEQ // CMP_STRIDE - 1) & (j[None, :] < SEQ // SLC_LEN)).astype(BF16)
    expand = (jnp.arange(SEQ)[None, :] // SLC_LEN == jnp.arange(LANES)[:, None]).astype(BF16)
    ks = jnp.arange(N_CMP_S)
    bias_cs = bias_of(PAST_LEN - (ks * CMP_STRIDE + CMP_LEN - 1))
    js = jnp.arange(N_SLC_S)
    cover_s = ((ks[:, None] * CMP_STRIDE < (js[None, :] + 1) * SLC_LEN) & (ks[:, None] * CMP_STRIDE + CMP_LEN > js[None, :] * SLC_LEN)
               & (ks[:, None] < N_CMP_S - 1) & (js[None, :] <= PAST_LEN // SLC_LEN)).astype(BF16)
    bias_w = bias_of(WINDOW - jnp.arange(WINDOW))
    bias_0 = bias_of(jnp.zeros((LANES,), jnp.int32))
    bias_s = bias_of(PAST_LEN - jnp.arange(PAST_LEN))
    return dict(t0=t0, t1=t1, t2=t2, bias_c=bias_c, cover_t=cover.T, expand=expand, bias_cs=bias_cs, cover_s=cover_s,
                bias_w=bias_w, bias_0=bias_0, bias_s=bias_s)


def _kv_export_kernel(c_ref, s_ref, w_ref, co_ref, so_ref, wo_ref):
    co_ref[...] = c_ref[...]
    so_ref[...] = s_ref[...]
    wo_ref[...] = w_ref[...]


def kv_export(z1):
    kv_w = 4 * LANES
    full = lambda c: pl.BlockSpec((SEQ, kv_w), lambda b: (b, c // kv_w))
    out_full = pl.BlockSpec((None, SEQ, kv_w), lambda b: (b, 0, 0))
    return pl.pallas_call(
        _kv_export_kernel, grid=(BATCH,),
        in_specs=[full(C_CMP), full(C_SLC),
                  pl.BlockSpec((WINDOW, kv_w), lambda b: ((b + 1) * (SEQ // WINDOW) - 1, C_WIN // kv_w))],
        out_specs=[out_full, out_full, pl.BlockSpec((None, WINDOW, kv_w), lambda b: (b, 0, 0))],
        out_shape=[jax.ShapeDtypeStruct((BATCH, SEQ, kv_w), F32)] * 2 + [jax.ShapeDtypeStruct((BATCH, WINDOW, kv_w), F32)],
        compiler_params=_cp(1), name="kv_export")(z1, z1, z1)


def _mixers(layer, z1, z2, z3, tabs, wts, st):
    new = {}
    sample = slice(M_PROMPT, M_PROMPT + DEC_BATCH)
    zs = z1[sample]

    def one_tile_per_batch(a):
        return jnp.zeros((DEC_BATCH, 8, a.shape[1]), F32).at[:, 0].set(a).reshape(DEC_BATCH * 8, a.shape[1])

    tail = jnp.zeros((M_SLAB - M_PROMPT - DEC_BATCH, W_MIX), BF16)
    slab = lambda p, s: jnp.concatenate([p, s, tail], axis=0)

    zero_halo = jnp.zeros((BATCH, CONV_HALO, W_MIX), F32)
    conv_args = (wts['conv_w'], wts['conv_b'], wts['conv_ln_g'], wts['conv_ln_b'])
    oa_p, cst_p = conv_mixer(z1, zero_halo, *conv_args, nb=BATCH, t_len=SEQ, tt=256, rb=32, n_last=256)
    halo_s = jnp.pad(st['conv'], ((0, 0), (CONV_HALO - (CONV_WIDTH - 1), 0), (0, 0)))
    oa_s, cst_s = conv_mixer(one_tile_per_batch(zs[:, :C_U]), halo_s, *conv_args, nb=DEC_BATCH, t_len=8, tt=8, rb=8, n_last=1)
    new['conv_p'] = cst_p[:, CONV_HALO - (CONV_WIDTH - 1):]
    new['conv_s'] = cst_s[:, CONV_HALO - (CONV_WIDTH - 1):]
    o_a = slab(oa_p, oa_s[::8])

    ws = wts['sgu_ws']
    w_causal = (ws * jnp.tril(jnp.ones((SGU_CHUNK, SGU_CHUNK), F32))).astype(BF16)
    bias_rows = jnp.repeat(wts['sgu_bs'].T, LANES, axis=1)
    ob_p = sgu_mixer(z1, w_causal, bias_rows, wts['sgu_ln_g'], wts['sgu_ln_b'], rows=M_PROMPT)
    w0 = jnp.repeat(ws[:, 0, 0], LANES)[None, :]
    ob_s, v_s = sgu_step(zs, w0, bias_rows[0:1], wts['sgu_ln_g'], wts['sgu_ln_b'])
    new['sgu_v'] = v_s[:, None, :]
    o_b = slab(ob_p, ob_s)

    w_a2p = jnp.zeros((LANES, GLA_KEYS), F32).at[:GLA_RANK].set(wts['gla_w_a2']).astype(BF16)
    gla_args = (w_a2p, wts['gla_b_a'], wts['gla_norm'])
    s0_p = jnp.zeros((BATCH, GLA_PAIRS, GLA_DV, LANES), F32)
    od_p, sp = gla_mixer(z2, z3, s0_p, *gla_args, nb=BATCH, t_len=SEQ, tt=256, chunk=GLA_CHUNK, n_valid=GLA_CHUNK)
    od_s, ss = gla_mixer(one_tile_per_batch(z2[sample]), one_tile_per_batch(z3[sample]), gla_state_in(st['gla']), *gla_args,
                         nb=DEC_BATCH, t_len=8, tt=8, chunk=8, n_valid=1)
    new['gla_p'] = gla_state_out(sp)
    new['gla_s'] = gla_state_out(ss)
    o_d = slab(od_p, od_s[::8])

    w_cmp = wts['nsa_w_cmp'].astype(BF16)
    cb_p = compress_prompt(z1, w_cmp)
    oc_p = nsa_prompt(z1, cb_p, tabs['bias_c'], tabs['t0'], tabs['t1'], tabs['t2'], tabs['cover_t'], tabs['expand'])
    n_pool = st['cmp'].shape[1]
    pt = st['page_table'] + layer * n_pool
    cache_cmp = st['cmp'].reshape(-1, PAGE_SIZE, 4 * LANES)
    cache_slc = st['slc'].reshape(-1, PAGE_SIZE, 4 * LANES)
    cb_s = compress_paged(cache_cmp, pt, w_cmp)
    q_s = zs[:, C_Q:C_Q + NSA_HEADS * LANES].reshape(DEC_BATCH, NSA_HEADS, LANES)
    kvw_new = zs[:, None, C_WIN:C_WIN + 4 * LANES]
    kvs_new = zs[:, None, C_SLC:C_SLC + 4 * LANES]
    win_buf = st['win'].reshape(DEC_BATCH, WINDOW, 4 * LANES)
    ocw, sel = nsa_step_a(q_s, cb_s, tabs['bias_cs'], tabs['cover_s'], win_buf, kvw_new, tabs['bias_w'], tabs['bias_0'])
    n_steps = N_PAGES // PAGES_PER_STEP
    blocks_per_step = CMP_ROWS // SLC_LEN
    sel_kv = jnp.repeat(sel[:, :KVH], GRP, axis=1)
    sel_r = sel_kv[:, :, :n_steps * blocks_per_step].reshape(DEC_BATCH, NSA_HEADS, n_steps, blocks_per_step)
    sel_r = jnp.pad(jnp.swapaxes(sel_r, 1, 2), ((0, 0), (0, 0), (0, 0), (0, LANES - blocks_per_step)))
    sel_new = jnp.broadcast_to(sel_kv[:, :, PAST_LEN // SLC_LEN, None], (DEC_BATCH, NSA_HEADS, LANES))
    gates_s = zs[:, C_CG:C_CG + 3 * NSA_HEADS].reshape(DEC_BATCH, NSA_HEADS, 3)
    gates_s = jnp.pad(gates_s, ((0, 0), (0, 0), (0, LANES - 3)))
    oc_s = nsa_step_b(cache_slc, pt, q_s, sel_r, tabs['expand'], tabs['bias_s'], kvs_new, sel_new, tabs['bias_0'], ocw, gates_s)
    o_c = slab(oc_p, oc_s.reshape(DEC_BATCH, W_MIX))

    cmp_p, slc_p, win_p = kv_export(z1)
    new['cmp_p'] = cmp_p.reshape(BATCH, SEQ // PAGE_SIZE, PAGE_SIZE, 2, KVH, HEAD_DIM)
    new['slc_p'] = slc_p.reshape(BATCH, SEQ // PAGE_SIZE, PAGE_SIZE, 2, KVH, HEAD_DIM)
    new['win_p'] = win_p.reshape(BATCH, WINDOW, 2, KVH, HEAD_DIM)
    kvs = lambda c: zs[:, c:c + 4 * LANES].reshape(DEC_BATCH, 1, 2, KVH, HEAD_DIM)
    new['cmp_s'] = kvs(C_CMP)
    new['slc_s'] = kvs(C_SLC)
    new['win_s'] = jnp.concatenate([st['win'][:, 1:], kvs(C_WIN)], axis=1)
    return (o_a, o_b, o_c, o_d), new


def kernel(x_prompt, x_sample, cache_cmp_kv, cache_slc_kv, state_win_kv, state_conv, state_gla, page_table, p_prompt, p_sample, ffn1_norm, ffn1_w_in, ffn1_w_out, mix_norm, w_in, conv_w, conv_b, conv_ln_g, conv_ln_b, conv_w_out, sgu_ln_g, sgu_ln_b, sgu_ws, sgu_bs, sgu_w_out, nsa_w_cmp, nsa_w_out, gla_w_a2, gla_b_a, gla_norm, gla_w_out, w_out, ffn2_norm, ffn2_w_in, ffn2_w_out, pe_norm, w_pe, w_pe_gate, rel_bias, final_norm):
    n_tail = M_SLAB - M_PROMPT - DEC_BATCH
    x = jnp.concatenate([x_prompt.reshape(M_PROMPT, D_MODEL), x_sample.reshape(DEC_BATCH, D_MODEL),
                         jnp.zeros((n_tail, D_MODEL), F32)], axis=0)
    tabs = _nsa_tables(rel_bias)
    delta = None
    news = []
    for i in range(DEPTH):
        x, h = add_norm(x, delta, ffn1_norm[i])
        delta = ffn_half(h, ffn1_w_in, ffn1_w_out, i)
        x, h = add_norm(x, delta, mix_norm[i])
        z1 = matmul_window(h, w_in, i, 0, Z1W)
        z2 = matmul_window(h, w_in, i, Z2_OFF, Z2W)
        z3 = matmul_window(h, w_in, i, Z3_OFF, Z3W)
        gates = matmul_window(h, w_in, i, G_OFF, N_BRANCH * D_MODEL, act="sigmoid", out_dtype=BF16)
        wts = dict(conv_w=conv_w[i], conv_b=conv_b[i], conv_ln_g=conv_ln_g[i], conv_ln_b=conv_ln_b[i],
                   sgu_ws=sgu_ws[i], sgu_bs=sgu_bs[i], sgu_ln_g=sgu_ln_g[i], sgu_ln_b=sgu_ln_b[i],
                   nsa_w_cmp=nsa_w_cmp[i], gla_w_a2=gla_w_a2[i], gla_b_a=gla_b_a[i], gla_norm=gla_norm[i])
        st = dict(cmp=cache_cmp_kv, slc=cache_slc_kv, win=state_win_kv[i], conv=state_conv[i], gla=state_gla[i],
                  page_table=page_table)
        branches, new = _mixers(i, z1, z2, z3, tabs, wts, st)
        news.append(new)
        merged = merge(branches, (conv_w_out, sgu_w_out, nsa_w_out, gla_w_out), gates, i)
        delta = matmul(merged, w_out, tn=512, layer=i)
        x, h = add_norm(x, delta, ffn2_norm[i])
        delta = ffn_half(h, ffn2_w_in, ffn2_w_out, i)
        x, h = add_norm(x, delta, pe_norm[i])
        ple = jnp.concatenate([p_prompt[i].reshape(M_PROMPT, PLE_DIM), p_sample[i].reshape(DEC_BATCH, PLE_DIM),
                               jnp.zeros((n_tail, PLE_DIM), F32)], axis=0).astype(BF16)
        delta = ple_delta(h, ple, w_pe_gate, w_pe, i)
    y_prompt = final_norm_rows(x, delta, final_norm, first_row=0, n_rows=M_PROMPT, tile=256)
    y_sample = final_norm_rows(x, delta, final_norm, first_row=M_PROMPT, n_rows=16, tile=16)[:DEC_BATCH]
    stack = lambda name: jnp.stack([n[name] for n in news], axis=0)
    return (y_prompt.reshape(BATCH, SEQ, D_MODEL), y_sample.reshape(DEC_BATCH, 1, D_MODEL),
            stack('cmp_p'), stack('cmp_s'), stack('slc_p'), stack('slc_s'), stack('win_p'), stack('win_s'),
            stack('conv_p'), stack('conv_s'), stack('gla_p'), stack('gla_s'), stack('sgu_v'))
```

```python
import functools
import math

import jax
import jax.numpy as jnp
from jax import lax
from jax.experimental import pallas as pl
from jax.experimental.pallas import tpu as pltpu

F32 = jnp.float32
BF16 = jnp.bfloat16

D_MODEL = 4096
BATCH = 4
SEQ = 2048
DEPTH = 2
DEC_BATCH = 8
PAST_LEN = 16384
PAGE_SIZE = 128
D_FF = 11008
W_MIX = 1024
N_BRANCH = 4
CONV_WIDTH = 31
SGU_CHUNK = 128
SGU_GROUPS = 8
HEAD_DIM = 128
NSA_HEADS = 8
KVH = 2
GRP = 4
CMP_STRIDE = 16
CMP_LEN = 32
SLC_LEN = 64
N_SELECT = 16
WINDOW = 512
Q_BLOCK = 128
FORCE_BONUS = 1.0e4
GLA_HEADS = 8
GLA_DV = 128
GLA_DK = 64
GLA_RANK = 16
GLA_TAU = 16.0
GLA_CHUNK = 64
REL_BUCKETS = 32
REL_MAX_DIST = 128
PLE_DIM = 256
EPS = 1e-6

LANES = 128
VMEM_LIMIT_BYTES = 56 * 1024 * 1024
M_PROMPT = BATCH * SEQ
M_SLAB = M_PROMPT + 16
TM = 912
TM_NORM = 304
FFN_CHUNK = 256
FFN_OUT_SLAB = 512
NEG = -1e30

C_VAL, C_GATE, C_U, C_V, C_Q, C_CMP, C_SLC, C_WIN, C_CG = 0, 1024, 2048, 3072, 4096, 5120, 5632, 6144, 6656
Z1W = 7168
Z2_OFF, Z2W = 6680, 2560
C2_DQ, C2_DK, C2_DV, C2_DA = 0, 512, 1024, 2048
Z3_OFF, Z3W = 8744, 1024
G_OFF = 9768

NT_DIMS = (((1,), (1,)), ((), ()))
TN_DIMS = (((0,), (0,)), ((), ()))


def _cp(n_axes, vmem=VMEM_LIMIT_BYTES):
    return pltpu.CompilerParams(dimension_semantics=("arbitrary",) * n_axes, vmem_limit_bytes=vmem)


def _dot(a, b):
    return jnp.dot(a, b, preferred_element_type=F32)


def _dot_nt(a, b):
    return lax.dot_general(a, b, NT_DIMS, preferred_element_type=F32)


def _dot_tn(a, b):
    return lax.dot_general(a, b, TN_DIMS, preferred_element_type=F32)


def _split_dot(x, w):
    hi = x.astype(BF16)
    lo = (x - hi.astype(F32)).astype(BF16)
    return _dot(hi, w) + _dot(lo, w)


def _silu(x):
    return x * jax.nn.sigmoid(x)


def _layernorm(x, g, b):
    mu = jnp.mean(x, axis=-1, keepdims=True)
    xc = x - mu
    var = jnp.mean(xc * xc, axis=-1, keepdims=True)
    return xc * lax.rsqrt(var + EPS) * g + b


def _masked_softmax(s, mask):
    s = jnp.where(mask, s, NEG)
    m = jnp.max(s, axis=-1, keepdims=True)
    p = jnp.where(mask, jnp.exp(s - m), 0.0)
    return p / jnp.maximum(jnp.sum(p, axis=-1, keepdims=True), 1e-30)


def _norm_kernel(*refs, has_delta, write_x):
    it = iter(refs)
    x_ref = next(it)
    d_ref = next(it) if has_delta else None
    g_ref = next(it)
    xo_ref = next(it) if write_x else None
    h_ref = next(it)
    x = x_ref[...]
    if has_delta:
        x = x + d_ref[...]
    if write_x:
        xo_ref[...] = x
    y = x * lax.rsqrt(jnp.mean(x * x, axis=-1, keepdims=True) + EPS) * g_ref[...]
    h_ref[...] = y.astype(h_ref.dtype)


def add_norm(x, delta, g, *, write_x=True, out_dtype=BF16):
    m, d = x.shape
    row = pl.BlockSpec((TM_NORM, d), lambda i: (i, 0))
    ins = [x] + ([delta] if delta is not None else []) + [g.reshape(1, d)]
    in_specs = [row] * (len(ins) - 1) + [pl.BlockSpec((1, d), lambda i: (0, 0))]
    out_shape = ([jax.ShapeDtypeStruct((m, d), F32)] if write_x else []) + [jax.ShapeDtypeStruct((m, d), out_dtype)]
    outs = pl.pallas_call(
        functools.partial(_norm_kernel, has_delta=delta is not None, write_x=write_x),
        grid=(m // TM_NORM,), in_specs=in_specs, out_specs=[row] * len(out_shape), out_shape=out_shape,
        compiler_params=_cp(1), name="add_norm")(*ins)
    return (outs[0], outs[1]) if write_x else (None, outs[0])


def final_norm_rows(x, delta, g, *, first_row, n_rows, tile):
    d = x.shape[1]
    first = first_row // tile
    row_in = pl.BlockSpec((tile, d), lambda i: (first + i, 0))
    return pl.pallas_call(
        functools.partial(_norm_kernel, has_delta=True, write_x=False), grid=(n_rows // tile,),
        in_specs=[row_in, row_in, pl.BlockSpec((1, d), lambda i: (0, 0))],
        out_specs=pl.BlockSpec((tile, d), lambda i: (i, 0)), out_shape=jax.ShapeDtypeStruct((n_rows, d), F32),
        compiler_params=_cp(1), name="final_norm")(x, delta, g.reshape(1, d))


def _ffn_kernel(h_ref, wg_ref, wu_ref, wo_ref, o_ref):
    @pl.when(pl.program_id(1) == 0)
    def _():
        o_ref[...] = jnp.zeros(o_ref.shape, F32)

    h = h_ref[...]
    g = _dot(h, wg_ref[...].astype(BF16))
    u = _dot(h, wu_ref[...].astype(BF16))
    a = (0.5 * _silu(g) * u).astype(BF16)
    for c0 in range(0, o_ref.shape[1], FFN_OUT_SLAB):
        cols = slice(c0, c0 + FFN_OUT_SLAB)
        o_ref[:, cols] += _dot(a, wo_ref[:, cols].astype(BF16))


def ffn_half(h, w_in, w_out, layer):
    m, d = h.shape
    nf = D_FF // FFN_CHUNK
    once = pl.Buffered(1)
    return pl.pallas_call(
        _ffn_kernel, grid=(m // TM, nf),
        in_specs=[pl.BlockSpec((TM, d), lambda i, f: (i, 0), pipeline_mode=once),
                  pl.BlockSpec((None, d, FFN_CHUNK), lambda i, f: (layer, 0, f)),
                  pl.BlockSpec((None, d, FFN_CHUNK), lambda i, f: (layer, 0, f + nf)),
                  pl.BlockSpec((None, FFN_CHUNK, d), lambda i, f: (layer, f, 0))],
        out_specs=pl.BlockSpec((TM, d), lambda i, f: (i, 0), pipeline_mode=once),
        out_shape=jax.ShapeDtypeStruct((m, d), F32), compiler_params=_cp(2), name="ffn")(h, w_in, w_in, w_out)


def _resident_bf16(w_ref, wb_ref):
    @pl.when(pl.program_id(1) == 0)
    def _():
        wb_ref[...] = w_ref[...].astype(BF16)

    return wb_ref[...]


def _mm_kernel(x_ref, w_ref, o_ref, *scratch, act):
    w = _resident_bf16(w_ref, scratch[0]) if scratch else w_ref[...]
    acc = _dot(x_ref[...], w)
    if act == "sigmoid":
        acc = jax.nn.sigmoid(acc)
    o_ref[...] = acc.astype(o_ref.dtype)


def _layer_block(shape, layer, index):
    return pl.BlockSpec((None,) + shape, lambda j, i: (layer,) + index(j, i))


def matmul(x, w, *, tn, layer=None, act=None, out_dtype=F32):
    m, k = x.shape
    n = w.shape[-1]
    if layer is None:
        w_spec, scratch = pl.BlockSpec((k, tn), lambda j, i: (0, j)), []
    else:
        w_spec, scratch = _layer_block((k, tn), layer, lambda j, i: (0, j)), [pltpu.VMEM((k, tn), BF16)]
    return pl.pallas_call(
        functools.partial(_mm_kernel, act=act), grid=(n // tn, m // TM),
        in_specs=[pl.BlockSpec((TM, k), lambda j, i: (i, 0)), w_spec],
        out_specs=pl.BlockSpec((TM, tn), lambda j, i: (i, j)), scratch_shapes=scratch,
        out_shape=jax.ShapeDtypeStruct((m, n), out_dtype), compiler_params=_cp(2), name="matmul")(x, w)


COL_TILE = 512


SUBLANES = 8


def _mm_window_kernel(x_ref, wt_ref, o_ref, wb_ref, *, act):
    @pl.when(pl.program_id(1) == 0)
    def _():
        wb_ref[...] = wt_ref[...].astype(BF16)

    acc = _dot_nt(x_ref[...], wb_ref[...])
    if act == "sigmoid":
        acc = jax.nn.sigmoid(acc)
    o_ref[...] = acc.astype(o_ref.dtype)


def matmul_window(x, w_t, layer, first_col, n_cols, *, act=None, out_dtype=F32):
    m, k = x.shape
    n_all = w_t.shape[1]
    first_row = layer * n_all + first_col
    assert first_row % SUBLANES == 0 and n_cols % COL_TILE == 0
    return pl.pallas_call(
        functools.partial(_mm_window_kernel, act=act), grid=(n_cols // COL_TILE, m // TM),
        in_specs=[pl.BlockSpec((TM, k), lambda j, i: (i, 0)),
                  pl.BlockSpec((pl.Element(COL_TILE), pl.Element(k)),
                               lambda j, i: (pl.multiple_of(first_row + j * COL_TILE, SUBLANES), 0))],
        out_specs=pl.BlockSpec((TM, COL_TILE), lambda j, i: (i, j)), scratch_shapes=[pltpu.VMEM((COL_TILE, k), BF16)],
        out_shape=jax.ShapeDtypeStruct((m, n_cols), out_dtype), compiler_params=_cp(2), name="matmul_window")(
            x, w_t.reshape(-1, k))


def _ple_kernel(h_ref, p_ref, wg_ref, wp_ref, o_ref, wgb_ref):
    gate = jax.nn.sigmoid(_dot(h_ref[...], _resident_bf16(wg_ref, wgb_ref)))
    o_ref[...] = gate * _dot(p_ref[...], wp_ref[...].astype(BF16))


def ple_delta(h, ple, w_gate, w_pe, layer, *, tn=512):
    m, k = h.shape
    n = w_gate.shape[-1]
    return pl.pallas_call(
        _ple_kernel, grid=(n // tn, m // TM),
        in_specs=[pl.BlockSpec((TM, k), lambda j, i: (i, 0)), pl.BlockSpec((TM, PLE_DIM), lambda j, i: (i, 0)),
                  _layer_block((k, tn), layer, lambda j, i: (0, j)),
                  _layer_block((PLE_DIM, tn), layer, lambda j, i: (0, j))],
        out_specs=pl.BlockSpec((TM, tn), lambda j, i: (i, j)), scratch_shapes=[pltpu.VMEM((k, tn), BF16)],
        out_shape=jax.ShapeDtypeStruct((m, n), F32), compiler_params=_cp(2), name="ple")(h, ple, w_gate, w_pe)


def _merge_kernel(*refs):
    o_refs, w_refs, g_refs, out_ref, wb_refs = refs[0:4], refs[4:8], refs[8:12], refs[12], refs[13:17]
    acc = None
    for o_ref, w_ref, g_ref, wb_ref in zip(o_refs, w_refs, g_refs, wb_refs):
        term = g_ref[...].astype(F32) * _dot(o_ref[...], _resident_bf16(w_ref, wb_ref))
        acc = term if acc is None else acc + term
    out_ref[...] = acc.astype(out_ref.dtype)


def merge(branches, weights, gates, layer, *, tn=512):
    m = branches[0].shape[0]
    nb = D_MODEL // tn
    in_specs = ([pl.BlockSpec((TM, W_MIX), lambda j, i: (i, 0))] * 4
                + [_layer_block((W_MIX, tn), layer, lambda j, i: (0, j))] * 4
                + [pl.BlockSpec((TM, tn), lambda j, i, b=b: (i, b * nb + j)) for b in range(4)])
    return pl.pallas_call(
        _merge_kernel, grid=(nb, m // TM), in_specs=in_specs,
        out_specs=pl.BlockSpec((TM, tn), lambda j, i: (i, j)), scratch_shapes=[pltpu.VMEM((W_MIX, tn), BF16)] * 4,
        out_shape=jax.ShapeDtypeStruct((m, D_MODEL), BF16), compiler_params=_cp(2), name="merge")(
            *branches, *weights, gates, gates, gates, gates)


CONV_HALO = 32


def _conv_kernel(val_ref, gate_ref, halo_ref, w_ref, cb_ref, lg_ref, lb_ref, o_ref, st_ref, aext, *, tt, rb, n_last):
    t = pl.program_id(1)

    @pl.when(t == 0)
    def _():
        aext[0:CONV_HALO, :] = halo_ref[...]

    aext[CONV_HALO:CONV_HALO + tt, :] = val_ref[...] * jax.nn.sigmoid(gate_ref[...])
    first = CONV_HALO - (CONV_WIDTH - 1)
    for r0 in range(0, tt, rb):
        acc = jnp.zeros((rb, W_MIX), F32) + cb_ref[...]
        for j in range(CONV_WIDTH):
            acc = acc + aext[r0 + first + j:r0 + first + j + rb, :] * w_ref[j:j + 1, :]
        y = _silu(_layernorm(acc, lg_ref[...], lb_ref[...]))
        o_ref[r0:r0 + rb, :] = y.astype(o_ref.dtype)

    @pl.when(t == pl.num_programs(1) - 1)
    def _():
        st_ref[...] = aext[n_last:n_last + CONV_HALO, :]

    if tt >= CONV_HALO:
        aext[0:CONV_HALO, :] = aext[tt:tt + CONV_HALO, :]


def conv_mixer(z, halo, conv_w, conv_b, ln_g, ln_b, *, nb, t_len, tt, rb, n_last):
    nt = t_len // tt
    w = jnp.zeros((CONV_HALO, W_MIX), F32).at[:CONV_WIDTH].set(conv_w)
    vec = pl.BlockSpec((1, W_MIX), lambda b, t: (0, 0))
    return pl.pallas_call(
        functools.partial(_conv_kernel, tt=tt, rb=rb, n_last=n_last), grid=(nb, nt),
        in_specs=[pl.BlockSpec((tt, W_MIX), lambda b, t: (b * nt + t, C_VAL // W_MIX)),
                  pl.BlockSpec((tt, W_MIX), lambda b, t: (b * nt + t, C_GATE // W_MIX)),
                  pl.BlockSpec((None, CONV_HALO, W_MIX), lambda b, t: (b, 0, 0)),
                  pl.BlockSpec((CONV_HALO, W_MIX), lambda b, t: (0, 0)), vec, vec, vec],
        out_specs=[pl.BlockSpec((tt, W_MIX), lambda b, t: (b * nt + t, 0)),
                   pl.BlockSpec((None, CONV_HALO, W_MIX), lambda b, t: (b, 0, 0))],
        out_shape=[jax.ShapeDtypeStruct((nb * t_len, W_MIX), BF16), jax.ShapeDtypeStruct((nb, CONV_HALO, W_MIX), F32)],
        scratch_shapes=[pltpu.VMEM((CONV_HALO + tt, W_MIX), F32)],
        compiler_params=_cp(2), name="conv")(z, z, halo, w, conv_b.reshape(1, -1), ln_g.reshape(1, -1), ln_b.reshape(1, -1))


def _sgu_kernel(u_ref, v_ref, wc_ref, bs_ref, lg_ref, lb_ref, o_ref, *, tt):
    for c0 in range(0, tt, SGU_CHUNK):
        rows = slice(c0, c0 + SGU_CHUNK)
        u = jax.nn.gelu(u_ref[rows, :])
        v = _layernorm(jax.nn.gelu(v_ref[rows, :]), lg_ref[...], lb_ref[...]).astype(BF16)
        parts = [_dot(wc_ref[g], v[:, g * LANES:(g + 1) * LANES]) for g in range(SGU_GROUPS)]
        mix = jnp.concatenate(parts, axis=1) + bs_ref[...]
        o_ref[rows, :] = (u * mix).astype(o_ref.dtype)


def sgu_mixer(z, w_causal, bias_rows, ln_g, ln_b, *, rows, tt=256):
    vec = pl.BlockSpec((1, W_MIX), lambda i: (0, 0))
    return pl.pallas_call(
        functools.partial(_sgu_kernel, tt=tt), grid=(rows // tt,),
        in_specs=[pl.BlockSpec((tt, W_MIX), lambda i: (i, C_U // W_MIX)),
                  pl.BlockSpec((tt, W_MIX), lambda i: (i, C_V // W_MIX)),
                  pl.BlockSpec((SGU_GROUPS, SGU_CHUNK, SGU_CHUNK), lambda i: (0, 0, 0)),
                  pl.BlockSpec((SGU_CHUNK, W_MIX), lambda i: (0, 0)), vec, vec],
        out_specs=pl.BlockSpec((tt, W_MIX), lambda i: (i, 0)),
        out_shape=jax.ShapeDtypeStruct((rows, W_MIX), BF16), compiler_params=_cp(1), name="sgu")(
            z, z, w_causal, bias_rows, ln_g.reshape(1, -1), ln_b.reshape(1, -1))


def _sgu_step_kernel(u_ref, v_ref, w0_ref, b0_ref, lg_ref, lb_ref, o_ref, vo_ref):
    u = jax.nn.gelu(u_ref[...])
    v = _layernorm(jax.nn.gelu(v_ref[...]), lg_ref[...], lb_ref[...])
    vo_ref[...] = v
    o_ref[...] = (u * (v * w0_ref[...] + b0_ref[...])).astype(o_ref.dtype)


def sgu_step(zs, w0, b0, ln_g, ln_b):
    n = zs.shape[0]
    vec = pl.BlockSpec((1, W_MIX), lambda i: (0, 0))
    return pl.pallas_call(
        _sgu_step_kernel, grid=(1,),
        in_specs=[pl.BlockSpec((n, W_MIX), lambda i: (0, C_U // W_MIX)),
                  pl.BlockSpec((n, W_MIX), lambda i: (0, C_V // W_MIX)), vec, vec, vec, vec],
        out_specs=[pl.BlockSpec((n, W_MIX), lambda i: (0, 0))] * 2,
        out_shape=[jax.ShapeDtypeStruct((n, W_MIX), BF16), jax.ShapeDtypeStruct((n, W_MIX), F32)],
        compiler_params=_cp(1), name="sgu_step")(zs, zs, w0, b0, ln_g.reshape(1, -1), ln_b.reshape(1, -1))


def _log_sigmoid(x):
    return jnp.minimum(x, 0.0) - jnp.log1p(jnp.exp(-jnp.abs(x)))


GLA_PAIRS = GLA_HEADS // 2
GLA_KEYS = GLA_HEADS * GLA_DK


def _gla_kernel(q_ref, k_ref, v_ref, r_ref, da_ref, wa_ref, ba_ref, gn_ref, s0_ref, o_ref, so_ref, st, *, tt, chunk, n_valid):
    t = pl.program_id(1)

    @pl.when(t == 0)
    def _():
        st[...] = s0_ref[...]

    row = lax.broadcasted_iota(jnp.int32, (chunk, chunk), 0)
    col = lax.broadcasted_iota(jnp.int32, (chunk, chunk), 1)
    causal = row >= col
    tril = causal.astype(BF16)
    low_half = lax.broadcasted_iota(jnp.int32, (chunk, LANES), 1) < GLA_DK
    for c0 in range(0, tt, chunk):
        rows = slice(c0, c0 + chunk)
        la = _log_sigmoid(_dot(da_ref[rows, :].astype(BF16), wa_ref[...]) + ba_ref[...]) * (1.0 / GLA_TAU)
        if n_valid < chunk:
            la = jnp.where(lax.broadcasted_iota(jnp.int32, la.shape, 0) < n_valid, la, 0.0)
        b = _split_dot_left(tril, la)
        bl = b[chunk - 1:chunk, :]
        kk = k_ref[rows, :]
        q_in = q_ref[rows, :] * (GLA_DK ** -0.5) * jnp.exp(b)
        k_in = (kk * jnp.exp(-b)).astype(BF16)
        k_end = kk * jnp.exp(bl - b)
        decay = jnp.exp(bl)
        vv = v_ref[rows, :].astype(BF16)
        for p in range(GLA_PAIRS):
            ps = slice(p * LANES, (p + 1) * LANES)
            state = st[p]
            state_b = state.astype(BF16)
            update = state * decay[:, ps]
            for e in range(2):
                hs = slice((2 * p + e) * GLA_DV, (2 * p + e + 1) * GLA_DV)
                mine = low_half if e == 0 else jnp.logical_not(low_half)
                q_h = jnp.where(mine, q_in[:, ps], 0.0).astype(BF16)
                k_h = jnp.where(mine, k_end[:, ps], 0.0).astype(BF16)
                a = jnp.where(causal, _dot_nt(q_h, k_in[:, ps]), 0.0).astype(BF16)
                o = _dot(a, vv[:, hs]) + _dot_nt(q_h, state_b)
                update = update + _dot_tn(vv[:, hs], k_h)
                y = o * lax.rsqrt(jnp.mean(o * o, axis=-1, keepdims=True) + EPS) * gn_ref[...]
                o_ref[rows, hs] = (y * _silu(r_ref[rows, hs])).astype(o_ref.dtype)
            st[p] = update

    @pl.when(t == pl.num_programs(1) - 1)
    def _():
        so_ref[...] = st[...]


def _split_dot_left(w, x):
    hi = x.astype(BF16)
    lo = (x - hi.astype(F32)).astype(BF16)
    return _dot(w, hi) + _dot(w, lo)


def gla_state_in(s):
    nb = s.shape[0]
    s = s.reshape(nb, GLA_PAIRS, 2, GLA_DK, GLA_DV)
    return jnp.transpose(s, (0, 1, 4, 2, 3)).reshape(nb, GLA_PAIRS, GLA_DV, 2 * GLA_DK)


def gla_state_out(s):
    nb = s.shape[0]
    s = s.reshape(nb, GLA_PAIRS, GLA_DV, 2, GLA_DK)
    return jnp.transpose(s, (0, 1, 3, 4, 2)).reshape(nb, GLA_HEADS, GLA_DK, GLA_DV)


def gla_mixer(z2, z3, s0_t, w_a2p, b_a, g_norm, *, nb, t_len, tt, chunk, n_valid):
    nt = t_len // tt
    rows = lambda width, c: pl.BlockSpec((tt, width), lambda b, t: (b * nt + t, c // width))
    state = pl.BlockSpec((None, GLA_PAIRS, GLA_DV, LANES), lambda b, t: (b, 0, 0, 0))
    return pl.pallas_call(
        functools.partial(_gla_kernel, tt=tt, chunk=chunk, n_valid=n_valid), grid=(nb, nt),
        in_specs=[rows(GLA_KEYS, C2_DQ), rows(GLA_KEYS, C2_DK), rows(W_MIX, C2_DV), rows(W_MIX, 0), rows(LANES, C2_DA),
                  pl.BlockSpec((LANES, GLA_KEYS), lambda b, t: (0, 0)),
                  pl.BlockSpec((1, GLA_KEYS), lambda b, t: (0, 0)),
                  pl.BlockSpec((1, LANES), lambda b, t: (0, 0)), state],
        out_specs=[pl.BlockSpec((tt, W_MIX), lambda b, t: (b * nt + t, 0)), state],
        out_shape=[jax.ShapeDtypeStruct((nb * t_len, W_MIX), BF16),
                   jax.ShapeDtypeStruct((nb, GLA_PAIRS, GLA_DV, LANES), F32)],
        scratch_shapes=[pltpu.VMEM((GLA_PAIRS, GLA_DV, LANES), F32)],
        compiler_params=_cp(2), name="gla")(z2, z2, z2, z3, z2, w_a2p, b_a.reshape(1, -1), g_norm.reshape(1, -1), s0_t)


SEG_PER_STEP = 128
CMP_ROWS = SEG_PER_STEP * CMP_STRIDE


def _compress_kernel(*refs, paged):
    if paged:
        n_in = 2 * (PAGES_PER_STEP + 1)
        in_refs = refs[1:1 + n_in]
        w_ref, o_ref = refs[1 + n_in], refs[2 + n_in]
        xs = refs[3 + n_in:]
    else:
        x_ref, w_ref, o_ref = refs[0:3]
        xs = refs[3:]
    for ch in range(4):
        plane, head = divmod(ch, KVH)
        cols = slice(ch * LANES, (ch + 1) * LANES)
        if paged:
            for j in range(PAGES_PER_STEP):
                xs[ch][j * PAGE_SIZE:(j + 1) * PAGE_SIZE, :] = in_refs[2 * j + plane][:, head, :]
            xs[ch][CMP_ROWS:CMP_ROWS + CMP_STRIDE, :] = in_refs[2 * PAGES_PER_STEP + plane][0:CMP_STRIDE, head, :]
        else:
            xs[ch][0:CMP_ROWS, :] = x_ref[:, cols]
            xs[ch][CMP_ROWS:CMP_ROWS + CMP_STRIDE, :] = jnp.zeros((CMP_STRIDE, LANES), F32)
        acc = jnp.zeros((SEG_PER_STEP, LANES), F32)
        for r in range(CMP_LEN):
            x = xs[ch][pl.ds(r, SEG_PER_STEP, stride=CMP_STRIDE), :]
            acc = acc + _dot(x.astype(BF16), w_ref[plane, r])
        o_ref[:, cols] = acc


def compress_prompt(z, w_cmp):
    return pl.pallas_call(
        functools.partial(_compress_kernel, paged=False), grid=(BATCH,),
        in_specs=[pl.BlockSpec((CMP_ROWS, 4 * LANES), lambda b: (b, C_CMP // (4 * LANES))),
                  pl.BlockSpec((2, CMP_LEN, LANES, LANES), lambda b: (0, 0, 0, 0))],
        out_specs=pl.BlockSpec((None, SEG_PER_STEP, 4 * LANES), lambda b: (b, 0, 0)),
        out_shape=jax.ShapeDtypeStruct((BATCH, SEG_PER_STEP, 4 * LANES), F32),
        scratch_shapes=[pltpu.VMEM((CMP_ROWS + CMP_STRIDE, LANES), F32)] * 4,
        compiler_params=_cp(1), name="compress_prompt")(z, w_cmp)


PAGES_PER_STEP = CMP_ROWS // PAGE_SIZE
N_PAGES = PAST_LEN // PAGE_SIZE


def _page_plane_spec(layer, page_of):
    def spec(plane):
        return pl.BlockSpec((None, None, PAGE_SIZE, None, KVH, HEAD_DIM),
                            lambda b, c, pt: (layer, page_of(b, c, pt), 0, plane, 0, 0))
    return spec


def compress_paged(cache, page_table, w_cmp, layer):
    n_steps = N_PAGES // PAGES_PER_STEP
    specs = []
    for j in range(PAGES_PER_STEP + 1):
        spec = _page_plane_spec(layer, lambda b, c, pt, j=j: pt[b, jnp.minimum(c * PAGES_PER_STEP + j, N_PAGES - 1)])
        specs += [spec(0), spec(1)]
    grid_spec = pltpu.PrefetchScalarGridSpec(
        num_scalar_prefetch=1, grid=(DEC_BATCH, n_steps),
        in_specs=specs + [pl.BlockSpec((2, CMP_LEN, LANES, LANES), lambda b, c, pt: (0, 0, 0, 0))],
        out_specs=pl.BlockSpec((None, SEG_PER_STEP, 4 * LANES), lambda b, c, pt: (b, c, 0)),
        scratch_shapes=[pltpu.VMEM((CMP_ROWS + CMP_STRIDE, LANES), F32)] * 4)
    return pl.pallas_call(
        functools.partial(_compress_kernel, paged=True), grid_spec=grid_spec,
        out_shape=jax.ShapeDtypeStruct((DEC_BATCH, n_steps * SEG_PER_STEP, 4 * LANES), F32),
        compiler_params=_cp(2), name="compress_paged")(page_table, *([cache] * len(specs)), w_cmp)


def _top_select(score, n_lanes):
    lane = lax.broadcasted_iota(jnp.int32, score.shape, 1).astype(F32)
    sel = jnp.zeros(score.shape, F32)
    for _ in range(N_SELECT):
        m = jnp.max(score, axis=-1, keepdims=True)
        idx = jnp.min(jnp.where(score == m, lane, float(n_lanes)), axis=-1, keepdims=True)
        pick = lane == idx
        sel = jnp.where(pick, 1.0, sel)
        score = jnp.where(pick, -3e38, score)
    return sel


def _bias_tiles(delta_of_tile, n_tiles, t0, t1, t2):
    tiles = []
    for j in range(n_tiles):
        delta = delta_of_tile(j)
        tiles.append(jnp.where(delta == 0, t0, jnp.where(delta == 1, t1, t2)))
    return jnp.concatenate(tiles, axis=1)


WIN_BAND = WINDOW + Q_BLOCK


SLC_KEY_STEP = 512


def _nsa_prompt_kernel(q_ref, cbk_ref, cbv_ref, ks_ref, vs_ref, kw_ref, vw_ref, g_ref, bc_ref, t0_ref, t1_ref, t2_ref,
                       covt_ref, e_ref, o_ref, oslc_ref):
    h = pl.program_id(1)
    i = pl.program_id(2)
    q = q_ref[...] * (HEAD_DIM ** -0.5)
    qs = jnp.concatenate([q[:, g * LANES:(g + 1) * LANES] for g in range(GRP)], axis=0).astype(BF16)
    qpos = i * Q_BLOCK + lax.broadcasted_iota(jnp.int32, (Q_BLOCK, 1), 0)
    lane = lax.broadcasted_iota(jnp.int32, (Q_BLOCK, LANES), 1)

    sc = _dot_nt(qs, cbk_ref[...].astype(BF16))
    cmask = (lane * CMP_STRIDE + (CMP_LEN - 1) <= qpos) & (lane < SEQ // CMP_STRIDE - 1)
    p_c = [_masked_softmax(sc[g * Q_BLOCK:(g + 1) * Q_BLOCK] + bc_ref[g], cmask) for g in range(GRP)]
    o_cmp = _dot(jnp.concatenate(p_c, axis=0).astype(BF16), cbv_ref[...].astype(BF16))
    psum = p_c[0] + p_c[1] + p_c[2] + p_c[3]
    p_hi = psum.astype(BF16)
    p_lo = (psum - p_hi.astype(F32)).astype(BF16)
    imp_t = _dot_nt(covt_ref[...], p_hi) + _dot_nt(covt_ref[...], p_lo)

    n_blk = SEQ // SLC_LEN
    blk = lax.broadcasted_iota(jnp.int32, (n_blk, Q_BLOCK), 0)
    qpos_l = i * Q_BLOCK + lax.broadcasted_iota(jnp.int32, (n_blk, Q_BLOCK), 1)
    cur = lax.shift_right_logical(qpos_l, 6)
    valid = blk * SLC_LEN <= qpos_l
    forced = (blk == 0) | (blk == cur) | (blk == cur - 1)
    score = jnp.where(valid, imp_t[0:n_blk] + jnp.where(forced, FORCE_BONUS, 0.0), NEG)
    rank = jnp.zeros((n_blk, Q_BLOCK), F32)
    for other_blk in range(n_blk):
        other = score[other_blk:other_blk + 1, :]
        rank = rank + jnp.where((other > score) | ((other == score) & (blk > other_blk)), 1.0, 0.0)
    sel_t = jnp.where(valid & (rank < N_SELECT), 1.0, 0.0).astype(BF16)

    def selected(n_keys):
        kpos = lax.broadcasted_iota(jnp.int32, (Q_BLOCK, n_keys), 1)
        smask = (_dot_tn(sel_t, e_ref[0:n_blk, 0:n_keys]) > 0.5) & (kpos <= qpos)
        ss = _dot_nt(qs, ks_ref[0:n_keys, :].astype(BF16))
        p_s = []
        for g in range(GRP):
            bias = _bias_tiles(lambda j: i - j, n_keys // Q_BLOCK, t0_ref[g], t1_ref[g], t2_ref[g])
            p_s.append(_masked_softmax(ss[g * Q_BLOCK:(g + 1) * Q_BLOCK] + bias, smask))
        oslc_ref[...] = _dot(jnp.concatenate(p_s, axis=0).astype(BF16), vs_ref[0:n_keys, :].astype(BF16))

    tiles_per_step = SLC_KEY_STEP // Q_BLOCK
    for hi in range(tiles_per_step, SEQ // Q_BLOCK + 1, tiles_per_step):
        pl.when((i >= hi - tiles_per_step) & (i < hi))(functools.partial(selected, hi * Q_BLOCK))
    o_slc = oslc_ref[...]

    first_tile = jnp.maximum(i - WINDOW // Q_BLOCK, 0)
    start = pl.multiple_of(first_tile * Q_BLOCK, Q_BLOCK)
    dist = qpos - (start + lax.broadcasted_iota(jnp.int32, (Q_BLOCK, WIN_BAND), 1))
    wmask = (dist >= 0) & (dist <= WINDOW)
    sw = _dot_nt(qs, kw_ref[pl.ds(start, WIN_BAND), :].astype(BF16))
    p_w = []
    for g in range(GRP):
        bias = _bias_tiles(lambda j: i - first_tile - j, WIN_BAND // Q_BLOCK, t0_ref[g], t1_ref[g], t2_ref[g])
        p_w.append(_masked_softmax(sw[g * Q_BLOCK:(g + 1) * Q_BLOCK] + bias, wmask))
    o_win = _dot(jnp.concatenate(p_w, axis=0).astype(BF16), vw_ref[pl.ds(start, WIN_BAND), :].astype(BF16))

    gates = jax.nn.sigmoid(g_ref[...])
    for g in range(GRP):
        rows = slice(g * Q_BLOCK, (g + 1) * Q_BLOCK)
        base = (h * GRP + g) * 3
        gc = [jnp.sum(jnp.where(lane == base + br, gates, 0.0), axis=-1, keepdims=True) for br in range(3)]
        o = gc[0] * o_cmp[rows] + gc[1] * o_slc[rows] + gc[2] * o_win[rows]
        o_ref[:, g * LANES:(g + 1) * LANES] = o.astype(o_ref.dtype)


def nsa_prompt(z, cb, bias_c, t0, t1, t2, cover_t, expand):
    nq = SEQ // Q_BLOCK
    kv = lambda c: pl.BlockSpec((SEQ, LANES), lambda b, h, i, c=c: (b, c // LANES + h))
    cbs = lambda c: pl.BlockSpec((None, SEG_PER_STEP, LANES), lambda b, h, i, c=c: (b, 0, c + h))
    toe = pl.BlockSpec((GRP, Q_BLOCK, LANES), lambda b, h, i: (h, 0, 0))
    return pl.pallas_call(
        _nsa_prompt_kernel, grid=(BATCH, KVH, nq),
        in_specs=[pl.BlockSpec((Q_BLOCK, GRP * LANES), lambda b, h, i: (b * nq + i, C_Q // (GRP * LANES) + h)),
                  cbs(0), cbs(KVH), kv(C_SLC), kv(C_SLC + KVH * LANES), kv(C_WIN), kv(C_WIN + KVH * LANES),
                  pl.BlockSpec((Q_BLOCK, LANES), lambda b, h, i: (b * nq + i, C_CG // LANES)),
                  pl.BlockSpec((GRP, Q_BLOCK, LANES), lambda b, h, i: (h, i, 0)), toe, toe, toe,
                  pl.BlockSpec((LANES, LANES), lambda b, h, i: (0, 0)),
                  pl.BlockSpec((LANES, SEQ), lambda b, h, i: (0, 0))],
        out_specs=pl.BlockSpec((Q_BLOCK, GRP * LANES), lambda b, h, i: (b * nq + i, h)),
        scratch_shapes=[pltpu.VMEM((GRP * Q_BLOCK, LANES), F32)],
        out_shape=jax.ShapeDtypeStruct((M_PROMPT, W_MIX), BF16), compiler_params=_cp(3), name="nsa_prompt")(
            z, cb, cb, z, z, z, z, z, bias_c, t0, t1, t2, cover_t, expand)


N_CMP_S = PAST_LEN // CMP_STRIDE
N_SLC_S = 384


def _head_rows(shape):
    return lax.broadcasted_iota(jnp.int32, shape, 0) < GRP


def _nsa_step_a_kernel(q_ref, cb_ref, bc_ref, cov_ref, wb_ref, kvn_ref, bw_ref, bwn_ref, ocw_ref, sel_ref):
    q = q_ref[...] * (HEAD_DIM ** -0.5)
    qb = q.astype(BF16)
    cb = cb_ref[...]
    first = _head_rows((NSA_HEADS, N_CMP_S))
    s = jnp.where(first, _dot_nt(qb, cb[:, 0:LANES].astype(BF16)), _dot_nt(qb, cb[:, LANES:2 * LANES].astype(BF16)))
    kidx = lax.broadcasted_iota(jnp.int32, (NSA_HEADS, N_CMP_S), 1)
    p = _masked_softmax(s + bc_ref[...], kidx * CMP_STRIDE + (CMP_LEN - 1) <= PAST_LEN)
    pb = p.astype(BF16)
    o_cmp = jnp.where(_head_rows((NSA_HEADS, LANES)), _dot(pb, cb[:, 2 * LANES:3 * LANES].astype(BF16)),
                      _dot(pb, cb[:, 3 * LANES:4 * LANES].astype(BF16)))
    row = lax.broadcasted_iota(jnp.int32, (NSA_HEADS, N_CMP_S), 0)
    ps0 = jnp.sum(jnp.where(first, p, 0.0), axis=0, keepdims=True)
    ps1 = jnp.sum(jnp.where(first, 0.0, p), axis=0, keepdims=True)
    psum = jnp.where(row == 0, ps0, jnp.where(row == 1, ps1, 0.0))
    imp = _split_dot(psum, cov_ref[...])
    lane = lax.broadcasted_iota(jnp.int32, (NSA_HEADS, N_SLC_S), 1)
    cur = PAST_LEN // SLC_LEN
    valid = lane <= cur
    forced = (lane == 0) | (lane == cur) | (lane == cur - 1)
    score = jnp.where(valid, imp + jnp.where(forced, FORCE_BONUS, 0.0), NEG)
    sel_ref[...] = _top_select(score, N_SLC_S) * valid.astype(F32)

    wb = wb_ref[...]
    firstw = _head_rows((NSA_HEADS, WINDOW))
    sw = jnp.where(firstw, _dot_nt(qb, wb[:, 0:LANES].astype(BF16)), _dot_nt(qb, wb[:, LANES:2 * LANES].astype(BF16)))
    sw = sw + bw_ref[...]
    firstl = _head_rows((NSA_HEADS, LANES))
    kvn = kvn_ref[...]
    k_new = jnp.where(firstl, kvn[:, 0:LANES], kvn[:, LANES:2 * LANES])
    v_new = jnp.where(firstl, kvn[:, 2 * LANES:3 * LANES], kvn[:, 3 * LANES:4 * LANES])
    s_new = jnp.sum(q * k_new, axis=-1, keepdims=True) + bwn_ref[:, 0:1]
    m = jnp.maximum(jnp.max(sw, axis=-1, keepdims=True), s_new)
    pw = jnp.exp(sw - m)
    pn = jnp.exp(s_new - m)
    pwb = pw.astype(BF16)
    acc = jnp.where(firstl, _dot(pwb, wb[:, 2 * LANES:3 * LANES].astype(BF16)), _dot(pwb, wb[:, 3 * LANES:4 * LANES].astype(BF16)))
    o_win = (acc + pn * v_new) / (jnp.sum(pw, axis=-1, keepdims=True) + pn)
    ocw_ref[:, 0:LANES] = o_cmp
    ocw_ref[:, LANES:2 * LANES] = o_win


def nsa_step_a(q_s, cb_s, bias_c, cover, win_buf, kvw_new, bias_w, bias_wn):
    per_b = lambda *shape: pl.BlockSpec((None,) + shape, lambda b: (b,) + (0,) * len(shape))
    whole = lambda *shape: pl.BlockSpec(shape, lambda b: (0,) * len(shape))
    return pl.pallas_call(
        _nsa_step_a_kernel, grid=(DEC_BATCH,),
        in_specs=[per_b(NSA_HEADS, LANES), per_b(N_CMP_S, 4 * LANES), whole(NSA_HEADS, N_CMP_S), whole(N_CMP_S, N_SLC_S),
                  per_b(WINDOW, 4 * LANES), per_b(1, 4 * LANES), whole(NSA_HEADS, WINDOW), whole(NSA_HEADS, LANES)],
        out_specs=[per_b(NSA_HEADS, 2 * LANES), per_b(NSA_HEADS, N_SLC_S)],
        out_shape=[jax.ShapeDtypeStruct((DEC_BATCH, NSA_HEADS, 2 * LANES), F32),
                   jax.ShapeDtypeStruct((DEC_BATCH, NSA_HEADS, N_SLC_S), F32)],
        compiler_params=_cp(1), name="nsa_step_a")(q_s, cb_s, bias_c, cover, win_buf, kvw_new, bias_w, bias_wn)


def _nsa_step_b_kernel(*refs):
    planes = refs[1:1 + 2 * PAGES_PER_STEP]
    (q_ref, sel_ref, e_ref, bs_ref, kvn_ref, seln_ref, bn_ref, ocw_ref, g_ref, o_ref, m_sc, l_sc, acc_sc) = refs[1 + 2 * PAGES_PER_STEP:]
    c = pl.program_id(1)

    @pl.when(c == 0)
    def _():
        m_sc[...] = jnp.full(m_sc.shape, NEG, F32)
        l_sc[...] = jnp.zeros(l_sc.shape, F32)
        acc_sc[...] = jnp.zeros(acc_sc.shape, F32)

    q = q_ref[...] * (HEAD_DIM ** -0.5)
    qb = q.astype(BF16)
    kv = [jnp.concatenate([planes[2 * j + plane][:, head, :].astype(BF16) for j in range(PAGES_PER_STEP)], axis=0)
          for plane in range(2) for head in range(KVH)]
    first = _head_rows((NSA_HEADS, CMP_ROWS))
    firstl = _head_rows((NSA_HEADS, LANES))
    s = jnp.where(first, _dot_nt(qb, kv[0]), _dot_nt(qb, kv[1])) + bs_ref[...]
    mask = _dot(sel_ref[...].astype(BF16), e_ref[...]) > 0.5
    s = jnp.where(mask, s, NEG)
    m_old = m_sc[...]
    m_new = jnp.maximum(m_old, jnp.max(s, axis=-1, keepdims=True))
    alpha = jnp.exp(m_old - m_new)
    p = jnp.where(mask, jnp.exp(s - m_new), 0.0)
    pb = p.astype(BF16)
    l_sc[...] = alpha * l_sc[...] + jnp.sum(p, axis=-1, keepdims=True)
    acc_sc[...] = alpha * acc_sc[...] + jnp.where(firstl, _dot(pb, kv[2]), _dot(pb, kv[3]))
    m_sc[...] = m_new

    @pl.when(c == pl.num_programs(1) - 1)
    def _():
        kvn = kvn_ref[...]
        k_new = jnp.where(firstl, kvn[:, 0:LANES], kvn[:, LANES:2 * LANES])
        v_new = jnp.where(firstl, kvn[:, 2 * LANES:3 * LANES], kvn[:, 3 * LANES:4 * LANES])
        on = seln_ref[:, 0:1] > 0.5
        s_n = jnp.where(on, jnp.sum(q * k_new, axis=-1, keepdims=True) + bn_ref[:, 0:1], NEG)
        m_old = m_sc[...]
        m_fin = jnp.maximum(m_old, s_n)
        alpha = jnp.exp(m_old - m_fin)
        p_n = jnp.where(on, jnp.exp(s_n - m_fin), 0.0)
        l_fin = alpha * l_sc[...] + p_n
        o_slc = (alpha * acc_sc[...] + p_n * v_new) / jnp.maximum(l_fin, 1e-30)
        gates = jax.nn.sigmoid(g_ref[...])
        o = gates[:, 0:1] * ocw_ref[:, 0:LANES] + gates[:, 1:2] * o_slc + gates[:, 2:3] * ocw_ref[:, LANES:2 * LANES]
        o_ref[...] = o.astype(o_ref.dtype)


def nsa_step_b(cache, page_table, layer, q_s, sel_r, expand, bias_s, kvs_new, sel_new, bias_n, ocw, gates):
    n_steps = N_PAGES // PAGES_PER_STEP
    page_specs = []
    for j in range(PAGES_PER_STEP):
        spec = _page_plane_spec(layer, lambda b, c, pt, j=j: pt[b, c * PAGES_PER_STEP + j])
        page_specs += [spec(0), spec(1)]
    per_b = lambda *shape: pl.BlockSpec((None,) + shape, lambda b, c, pt: (b,) + (0,) * len(shape))
    whole = lambda *shape: pl.BlockSpec(shape, lambda b, c, pt: (0,) * len(shape))
    grid_spec = pltpu.PrefetchScalarGridSpec(
        num_scalar_prefetch=1, grid=(DEC_BATCH, n_steps),
        in_specs=page_specs + [
            per_b(NSA_HEADS, LANES),
            pl.BlockSpec((None, None, NSA_HEADS, LANES), lambda b, c, pt: (b, c, 0, 0)),
            whole(LANES, CMP_ROWS),
            pl.BlockSpec((NSA_HEADS, CMP_ROWS), lambda b, c, pt: (0, c)),
            per_b(1, 4 * LANES), per_b(NSA_HEADS, LANES), whole(NSA_HEADS, LANES),
            per_b(NSA_HEADS, 2 * LANES), per_b(NSA_HEADS, LANES)],
        out_specs=per_b(NSA_HEADS, LANES),
        scratch_shapes=[pltpu.VMEM((NSA_HEADS, 1), F32), pltpu.VMEM((NSA_HEADS, 1), F32), pltpu.VMEM((NSA_HEADS, LANES), F32)])
    return pl.pallas_call(
        _nsa_step_b_kernel, grid_spec=grid_spec,
        out_shape=jax.ShapeDtypeStruct((DEC_BATCH, NSA_HEADS, LANES), BF16),
        compiler_params=_cp(2), name="nsa_step_b")(
            page_table, *([cache] * len(page_specs)), q_s, sel_r, expand, bias_s, kvs_new, sel_new, bias_n, ocw, gates)


def _rel_bias_of(rel_bias, dist):
    n = jnp.maximum(dist, 0)
    max_exact = REL_BUCKETS // 2
    nf = jnp.maximum(n, 1).astype(F32)
    large = max_exact + (jnp.log(nf / max_exact) / math.log(REL_MAX_DIST / max_exact)
                         * (REL_BUCKETS - max_exact)).astype(jnp.int32)
    bucket = jnp.where(n < max_exact, n, jnp.minimum(large, REL_BUCKETS - 1))[None]
    out = jnp.zeros((NSA_HEADS,) + dist.shape, F32)
    for k in range(REL_BUCKETS):
        out = jnp.where(bucket == k, rel_bias[k].reshape((NSA_HEADS,) + (1,) * dist.ndim), out)
    return out


def _nsa_tables(rel_bias):
    bias_of = functools.partial(_rel_bias_of, rel_bias)
    r = jnp.arange(Q_BLOCK)
    diff = r[:, None] - r[None, :]
    t0 = bias_of(diff)
    t1 = bias_of(Q_BLOCK + diff)
    t2 = bias_of(2 * Q_BLOCK + diff)
    bias_c = bias_of(jnp.arange(SEQ)[:, None] - (jnp.arange(LANES)[None, :] * CMP_STRIDE + CMP_LEN - 1))
    k = jnp.arange(LANES)
    j = jnp.arange(LANES)
    cover = ((k[:, None] * CMP_STRIDE < (j[None, :] + 1) * SLC_LEN) & (k[:, None] * CMP_STRIDE + CMP_LEN > j[None, :] * SLC_LEN)
             & (k[:, None] < SEQ // CMP_STRIDE - 1) & (j[None, :] < SEQ // SLC_LEN)).astype(BF16)
    expand = (jnp.arange(SEQ)[None, :] // SLC_LEN == jnp.arange(LANES)[:, None]).astype(BF16)
    ks = jnp.arange(N_CMP_S)
    bias_cs = bias_of(PAST_LEN - (ks * CMP_STRIDE + CMP_LEN - 1))
    js = jnp.arange(N_SLC_S)
    cover_s = ((ks[:, None] * CMP_STRIDE < (js[None, :] + 1) * SLC_LEN) & (ks[:, None] * CMP_STRIDE + CMP_LEN > js[None, :] * SLC_LEN)
               & (ks[:, None] < N_CMP_S - 1) & (js[None, :] <= PAST_LEN // SLC_LEN)).astype(BF16)
    bias_w = bias_of(WINDOW - jnp.arange(WINDOW))
    bias_0 = bias_of(jnp.zeros((LANES,), jnp.int32))
    bias_s = bias_of(PAST_LEN - jnp.arange(PAST_LEN))
    return dict(t0=t0, t1=t1, t2=t2, bias_c=bias_c, cover_t=cover.T, expand=expand, bias_cs=bias_cs, cover_s=cover_s,
                bias_w=bias_w, bias_0=bias_0, bias_s=bias_s)


def _kv_export_kernel(c_ref, s_ref, w_ref, co_ref, so_ref, wo_ref):
    co_ref[...] = c_ref[...]
    so_ref[...] = s_ref[...]
    wo_ref[...] = w_ref[...]


def kv_export(z1):
    kv_w = 4 * LANES
    full = lambda c: pl.BlockSpec((SEQ, kv_w), lambda b: (b, c // kv_w))
    out_full = pl.BlockSpec((None, SEQ, kv_w), lambda b: (b, 0, 0))
    return pl.pallas_call(
        _kv_export_kernel, grid=(BATCH,),
        in_specs=[full(C_CMP), full(C_SLC),
                  pl.BlockSpec((WINDOW, kv_w), lambda b: ((b + 1) * (SEQ // WINDOW) - 1, C_WIN // kv_w))],
        out_specs=[out_full, out_full, pl.BlockSpec((None, WINDOW, kv_w), lambda b: (b, 0, 0))],
        out_shape=[jax.ShapeDtypeStruct((BATCH, SEQ, kv_w), F32)] * 2 + [jax.ShapeDtypeStruct((BATCH, WINDOW, kv_w), F32)],
        compiler_params=_cp(1), name="kv_export")(z1, z1, z1)


def _mixers(layer, z1, z2, z3, tabs, wts, st):
    new = {}
    sample = slice(M_PROMPT, M_PROMPT + DEC_BATCH)
    zs = z1[sample]

    def one_tile_per_batch(a):
        return jnp.zeros((DEC_BATCH, 8, a.shape[1]), F32).at[:, 0].set(a).reshape(DEC_BATCH * 8, a.shape[1])

    tail = jnp.zeros((M_SLAB - M_PROMPT - DEC_BATCH, W_MIX), BF16)
    slab = lambda p, s: jnp.concatenate([p, s, tail], axis=0)

    zero_halo = jnp.zeros((BATCH, CONV_HALO, W_MIX), F32)
    conv_args = (wts['conv_w'], wts['conv_b'], wts['conv_ln_g'], wts['conv_ln_b'])
    oa_p, cst_p = conv_mixer(z1, zero_halo, *conv_args, nb=BATCH, t_len=SEQ, tt=256, rb=32, n_last=256)
    halo_s = jnp.pad(st['conv'], ((0, 0), (CONV_HALO - (CONV_WIDTH - 1), 0), (0, 0)))
    oa_s, cst_s = conv_mixer(one_tile_per_batch(zs[:, :C_U]), halo_s, *conv_args, nb=DEC_BATCH, t_len=8, tt=8, rb=8, n_last=1)
    new['conv_p'] = cst_p[:, CONV_HALO - (CONV_WIDTH - 1):]
    new['conv_s'] = cst_s[:, CONV_HALO - (CONV_WIDTH - 1):]
    o_a = slab(oa_p, oa_s[::8])

    ws = wts['sgu_ws']
    w_causal = (ws * jnp.tril(jnp.ones((SGU_CHUNK, SGU_CHUNK), F32))).astype(BF16)
    bias_rows = jnp.repeat(wts['sgu_bs'].T, LANES, axis=1)
    ob_p = sgu_mixer(z1, w_causal, bias_rows, wts['sgu_ln_g'], wts['sgu_ln_b'], rows=M_PROMPT)
    w0 = jnp.repeat(ws[:, 0, 0], LANES)[None, :]
    ob_s, v_s = sgu_step(zs, w0, bias_rows[0:1], wts['sgu_ln_g'], wts['sgu_ln_b'])
    new['sgu_v'] = v_s[:, None, :]
    o_b = slab(ob_p, ob_s)

    w_a2p = jnp.zeros((LANES, GLA_KEYS), F32).at[:GLA_RANK].set(wts['gla_w_a2']).astype(BF16)
    gla_args = (w_a2p, wts['gla_b_a'], wts['gla_norm'])
    s0_p = jnp.zeros((BATCH, GLA_PAIRS, GLA_DV, LANES), F32)
    od_p, sp = gla_mixer(z2, z3, s0_p, *gla_args, nb=BATCH, t_len=SEQ, tt=256, chunk=GLA_CHUNK, n_valid=GLA_CHUNK)
    od_s, ss = gla_mixer(one_tile_per_batch(z2[sample]), one_tile_per_batch(z3[sample]), gla_state_in(st['gla']), *gla_args,
                         nb=DEC_BATCH, t_len=8, tt=8, chunk=8, n_valid=1)
    new['gla_p'] = gla_state_out(sp)
    new['gla_s'] = gla_state_out(ss)
    o_d = slab(od_p, od_s[::8])

    w_cmp = wts['nsa_w_cmp'].astype(BF16)
    cb_p = compress_prompt(z1, w_cmp)
    oc_p = nsa_prompt(z1, cb_p, tabs['bias_c'], tabs['t0'], tabs['t1'], tabs['t2'], tabs['cover_t'], tabs['expand'])
    pt = st['page_table']
    cb_s = compress_paged(st['cmp'], pt, w_cmp, layer)
    q_s = zs[:, C_Q:C_Q + NSA_HEADS * LANES].reshape(DEC_BATCH, NSA_HEADS, LANES)
    kvw_new = zs[:, None, C_WIN:C_WIN + 4 * LANES]
    kvs_new = zs[:, None, C_SLC:C_SLC + 4 * LANES]
    win_buf = st['win'].reshape(DEC_BATCH, WINDOW, 4 * LANES)
    ocw, sel = nsa_step_a(q_s, cb_s, tabs['bias_cs'], tabs['cover_s'], win_buf, kvw_new, tabs['bias_w'], tabs['bias_0'])
    n_steps = N_PAGES // PAGES_PER_STEP
    blocks_per_step = CMP_ROWS // SLC_LEN
    sel_kv = jnp.repeat(sel[:, :KVH], GRP, axis=1)
    sel_r = sel_kv[:, :, :n_steps * blocks_per_step].reshape(DEC_BATCH, NSA_HEADS, n_steps, blocks_per_step)
    sel_r = jnp.pad(jnp.swapaxes(sel_r, 1, 2), ((0, 0), (0, 0), (0, 0), (0, LANES - blocks_per_step)))
    sel_new = jnp.broadcast_to(sel_kv[:, :, PAST_LEN // SLC_LEN, None], (DEC_BATCH, NSA_HEADS, LANES))
    gates_s = zs[:, C_CG:C_CG + 3 * NSA_HEADS].reshape(DEC_BATCH, NSA_HEADS, 3)
    gates_s = jnp.pad(gates_s, ((0, 0), (0, 0), (0, LANES - 3)))
    oc_s = nsa_step_b(st['slc'], pt, layer, q_s, sel_r, tabs['expand'], tabs['bias_s'], kvs_new, sel_new, tabs['bias_0'], ocw, gates_s)
    o_c = slab(oc_p, oc_s.reshape(DEC_BATCH, W_MIX))

    cmp_p, slc_p, win_p = kv_export(z1)
    new['cmp_p'] = cmp_p.reshape(BATCH, SEQ // PAGE_SIZE, PAGE_SIZE, 2, KVH, HEAD_DIM)
    new['slc_p'] = slc_p.reshape(BATCH, SEQ // PAGE_SIZE, PAGE_SIZE, 2, KVH, HEAD_DIM)
    new['win_p'] = win_p.reshape(BATCH, WINDOW, 2, KVH, HEAD_DIM)
    kvs = lambda c: zs[:, c:c + 4 * LANES].reshape(DEC_BATCH, 1, 2, KVH, HEAD_DIM)
    new['cmp_s'] = kvs(C_CMP)
    new['slc_s'] = kvs(C_SLC)
    new['win_s'] = jnp.concatenate([st['win'][:, 1:], kvs(C_WIN)], axis=1)
    return (o_a, o_b, o_c, o_d), new


def kernel(x_prompt, x_sample, cache_cmp_kv, cache_slc_kv, state_win_kv, state_conv, state_gla, page_table, p_prompt, p_sample, ffn1_norm, ffn1_w_in, ffn1_w_out, mix_norm, w_in, conv_w, conv_b, conv_ln_g, conv_ln_b, conv_w_out, sgu_ln_g, sgu_ln_b, sgu_ws, sgu_bs, sgu_w_out, nsa_w_cmp, nsa_w_out, gla_w_a2, gla_b_a, gla_norm, gla_w_out, w_out, ffn2_norm, ffn2_w_in, ffn2_w_out, pe_norm, w_pe, w_pe_gate, rel_bias, final_norm):
    n_tail = M_SLAB - M_PROMPT - DEC_BATCH
    x = jnp.concatenate([x_prompt.reshape(M_PROMPT, D_MODEL), x_sample.reshape(DEC_BATCH, D_MODEL),
                         jnp.zeros((n_tail, D_MODEL), F32)], axis=0)
    tabs = _nsa_tables(rel_bias)
    w_in_t = jnp.swapaxes(w_in, 1, 2)
    delta = None
    news = []
    for i in range(DEPTH):
        x, h = add_norm(x, delta, ffn1_norm[i])
        delta = ffn_half(h, ffn1_w_in, ffn1_w_out, i)
        x, h = add_norm(x, delta, mix_norm[i])
        z1 = matmul_window(h, w_in_t, i, 0, Z1W)
        z2 = matmul_window(h, w_in_t, i, Z2_OFF, Z2W)
        z3 = matmul_window(h, w_in_t, i, Z3_OFF, Z3W)
        gates = matmul_window(h, w_in_t, i, G_OFF, N_BRANCH * D_MODEL, act="sigmoid", out_dtype=BF16)
        wts = dict(conv_w=conv_w[i], conv_b=conv_b[i], conv_ln_g=conv_ln_g[i], conv_ln_b=conv_ln_b[i],
                   sgu_ws=sgu_ws[i], sgu_bs=sgu_bs[i], sgu_ln_g=sgu_ln_g[i], sgu_ln_b=sgu_ln_b[i],
                   nsa_w_cmp=nsa_w_cmp[i], gla_w_a2=gla_w_a2[i], gla_b_a=gla_b_a[i], gla_norm=gla_norm[i])
        st = dict(cmp=cache_cmp_kv, slc=cache_slc_kv, win=state_win_kv[i], conv=state_conv[i], gla=state_gla[i],
                  page_table=page_table)
        branches, new = _mixers(i, z1, z2, z3, tabs, wts, st)
        news.append(new)
        merged = merge(branches, (conv_w_out, sgu_w_out, nsa_w_out, gla_w_out), gates, i)
        delta = matmul(merged, w_out, tn=512, layer=i)
        x, h = add_norm(x, delta, ffn2_norm[i])
        delta = ffn_half(h, ffn2_w_in, ffn2_w_out, i)
        x, h = add_norm(x, delta, pe_norm[i])
        ple = jnp.concatenate([p_prompt[i].reshape(M_PROMPT, PLE_DIM), p_sample[i].reshape(DEC_BATCH, PLE_DIM),
                               jnp.zeros((n_tail, PLE_DIM), F32)], axis=0).astype(BF16)
        delta = ple_delta(h, ple, w_pe_gate, w_pe, i)
    y_prompt = final_norm_rows(x, delta, final_norm, first_row=0, n_rows=M_PROMPT, tile=256)
    y_sample = final_norm_rows(x, delta, final_norm, first_row=M_PROMPT, n_rows=16, tile=16)[:DEC_BATCH]
    stack = lambda name: jnp.stack([n[name] for n in news], axis=0)
    return (y_prompt.reshape(BATCH, SEQ, D_MODEL), y_sample.reshape(DEC_BATCH, 1, D_MODEL),
            stack('cmp_p'), stack('cmp_s'), stack('slc_p'), stack('slc_s'), stack('win_p'), stack('win_s'),
            stack('conv_p'), stack('conv_s'), stack('gla_p'), stack('gla_s'), stack('sgu_v'))
```

```python
import functools
import math

import jax
import jax.numpy as jnp
from jax import lax
from jax.experimental import pallas as pl
from jax.experimental.pallas import tpu as pltpu

F32 = jnp.float32
BF16 = jnp.bfloat16

D_MODEL = 4096
BATCH = 4
SEQ = 2048
DEPTH = 2
DEC_BATCH = 8
PAST_LEN = 16384
PAGE_SIZE = 128
D_FF = 11008
W_MIX = 1024
N_BRANCH = 4
CONV_WIDTH = 31
SGU_CHUNK = 128
SGU_GROUPS = 8
HEAD_DIM = 128
NSA_HEADS = 8
KVH = 2
GRP = 4
CMP_STRIDE = 16
CMP_LEN = 32
SLC_LEN = 64
N_SELECT = 16
WINDOW = 512
Q_BLOCK = 128
FORCE_BONUS = 1.0e4
GLA_HEADS = 8
GLA_DV = 128
GLA_DK = 64
GLA_RANK = 16
GLA_TAU = 16.0
GLA_CHUNK = 64
REL_BUCKETS = 32
REL_MAX_DIST = 128
PLE_DIM = 256
EPS = 1e-6

LANES = 128
VMEM_LIMIT_BYTES = 56 * 1024 * 1024
M_PROMPT = BATCH * SEQ
M_SLAB = M_PROMPT + 16
TM = 912
TM_NORM = 304
FFN_CHUNK = 256
FFN_OUT_SLAB = 512
NEG = -1e30

C_VAL, C_GATE, C_U, C_V, C_Q, C_CMP, C_SLC, C_WIN, C_CG = 0, 1024, 2048, 3072, 4096, 5120, 5632, 6144, 6656
Z1W = 7168
Z2_OFF, Z2W = 6680, 2560
C2_DQ, C2_DK, C2_DV, C2_DA = 0, 512, 1024, 2048
Z3_OFF, Z3W = 8744, 1024
G_OFF = 9768

NT_DIMS = (((1,), (1,)), ((), ()))
TN_DIMS = (((0,), (0,)), ((), ()))


def _cp(n_axes, vmem=VMEM_LIMIT_BYTES):
    return pltpu.CompilerParams(dimension_semantics=("arbitrary",) * n_axes, vmem_limit_bytes=vmem)


def _dot(a, b):
    return jnp.dot(a, b, preferred_element_type=F32)


def _dot_nt(a, b):
    return lax.dot_general(a, b, NT_DIMS, preferred_element_type=F32)


def _dot_tn(a, b):
    return lax.dot_general(a, b, TN_DIMS, preferred_element_type=F32)


def _split_dot(x, w):
    hi = x.astype(BF16)
    lo = (x - hi.astype(F32)).astype(BF16)
    return _dot(hi, w) + _dot(lo, w)


def _silu(x):
    return x * jax.nn.sigmoid(x)


def _layernorm(x, g, b):
    mu = jnp.mean(x, axis=-1, keepdims=True)
    xc = x - mu
    var = jnp.mean(xc * xc, axis=-1, keepdims=True)
    return xc * lax.rsqrt(var + EPS) * g + b


def _masked_softmax(s, mask):
    s = jnp.where(mask, s, NEG)
    m = jnp.max(s, axis=-1, keepdims=True)
    p = jnp.where(mask, jnp.exp(s - m), 0.0)
    return p / jnp.maximum(jnp.sum(p, axis=-1, keepdims=True), 1e-30)


def _norm_kernel(*refs, has_delta, write_x):
    it = iter(refs)
    x_ref = next(it)
    d_ref = next(it) if has_delta else None
    g_ref = next(it)
    xo_ref = next(it) if write_x else None
    h_ref = next(it)
    x = x_ref[...]
    if has_delta:
        x = x + d_ref[...]
    if write_x:
        xo_ref[...] = x
    y = x * lax.rsqrt(jnp.mean(x * x, axis=-1, keepdims=True) + EPS) * g_ref[...]
    h_ref[...] = y.astype(h_ref.dtype)


def add_norm(x, delta, g, *, write_x=True, out_dtype=BF16):
    m, d = x.shape
    row = pl.BlockSpec((TM_NORM, d), lambda i: (i, 0))
    ins = [x] + ([delta] if delta is not None else []) + [g.reshape(1, d)]
    in_specs = [row] * (len(ins) - 1) + [pl.BlockSpec((1, d), lambda i: (0, 0))]
    out_shape = ([jax.ShapeDtypeStruct((m, d), F32)] if write_x else []) + [jax.ShapeDtypeStruct((m, d), out_dtype)]
    outs = pl.pallas_call(
        functools.partial(_norm_kernel, has_delta=delta is not None, write_x=write_x),
        grid=(m // TM_NORM,), in_specs=in_specs, out_specs=[row] * len(out_shape), out_shape=out_shape,
        compiler_params=_cp(1), name="add_norm")(*ins)
    return (outs[0], outs[1]) if write_x else (None, outs[0])


def final_norm_rows(x, delta, g, *, first_row, n_rows, tile):
    d = x.shape[1]
    first = first_row // tile
    row_in = pl.BlockSpec((tile, d), lambda i: (first + i, 0))
    return pl.pallas_call(
        functools.partial(_norm_kernel, has_delta=True, write_x=False), grid=(n_rows // tile,),
        in_specs=[row_in, row_in, pl.BlockSpec((1, d), lambda i: (0, 0))],
        out_specs=pl.BlockSpec((tile, d), lambda i: (i, 0)), out_shape=jax.ShapeDtypeStruct((n_rows, d), F32),
        compiler_params=_cp(1), name="final_norm")(x, delta, g.reshape(1, d))


def _ffn_kernel(h_ref, wg_ref, wu_ref, wo_ref, o_ref):
    @pl.when(pl.program_id(1) == 0)
    def _():
        o_ref[...] = jnp.zeros(o_ref.shape, F32)

    h = h_ref[...]
    g = _dot(h, wg_ref[...].astype(BF16))
    u = _dot(h, wu_ref[...].astype(BF16))
    a = (0.5 * _silu(g) * u).astype(BF16)
    for c0 in range(0, o_ref.shape[1], FFN_OUT_SLAB):
        cols = slice(c0, c0 + FFN_OUT_SLAB)
        o_ref[:, cols] += _dot(a, wo_ref[:, cols].astype(BF16))


def ffn_half(h, w_in, w_out, layer):
    m, d = h.shape
    nf = D_FF // FFN_CHUNK
    once = pl.Buffered(1)
    return pl.pallas_call(
        _ffn_kernel, grid=(m // TM, nf),
        in_specs=[pl.BlockSpec((TM, d), lambda i, f: (i, 0), pipeline_mode=once),
                  pl.BlockSpec((None, d, FFN_CHUNK), lambda i, f: (layer, 0, f)),
                  pl.BlockSpec((None, d, FFN_CHUNK), lambda i, f: (layer, 0, f + nf)),
                  pl.BlockSpec((None, FFN_CHUNK, d), lambda i, f: (layer, f, 0))],
        out_specs=pl.BlockSpec((TM, d), lambda i, f: (i, 0), pipeline_mode=once),
        out_shape=jax.ShapeDtypeStruct((m, d), F32), compiler_params=_cp(2), name="ffn")(h, w_in, w_in, w_out)


def _resident_bf16(w_ref, wb_ref):
    @pl.when(pl.program_id(1) == 0)
    def _():
        wb_ref[...] = w_ref[...].astype(BF16)

    return wb_ref[...]


def _mm_kernel(x_ref, w_ref, o_ref, *scratch, act):
    w = _resident_bf16(w_ref, scratch[0]) if scratch else w_ref[...]
    acc = _dot(x_ref[...], w)
    if act == "sigmoid":
        acc = jax.nn.sigmoid(acc)
    o_ref[...] = acc.astype(o_ref.dtype)


def _layer_block(shape, layer, index, *, single_buffer=False):
    mode = dict(pipeline_mode=pl.Buffered(1)) if single_buffer else {}
    return pl.BlockSpec((None,) + shape, lambda j, i: (layer,) + index(j, i), **mode)


def matmul(x, w, *, tn, layer=None, act=None, out_dtype=F32):
    m, k = x.shape
    n = w.shape[-1]
    if layer is None:
        w_spec, scratch = pl.BlockSpec((k, tn), lambda j, i: (0, j)), []
    else:
        w_spec = _layer_block((k, tn), layer, lambda j, i: (0, j), single_buffer=True)
        scratch = [pltpu.VMEM((k, tn), BF16)]
    return pl.pallas_call(
        functools.partial(_mm_kernel, act=act), grid=(n // tn, m // TM),
        in_specs=[pl.BlockSpec((TM, k), lambda j, i: (i, 0)), w_spec],
        out_specs=pl.BlockSpec((TM, tn), lambda j, i: (i, j)), scratch_shapes=scratch,
        out_shape=jax.ShapeDtypeStruct((m, n), out_dtype), compiler_params=_cp(2), name="matmul")(x, w)


COL_TILE = 512


SUBLANES = 8


def _mm_window_kernel(x_ref, wt_ref, o_ref, wb_ref, *, act):
    @pl.when(pl.program_id(1) == 0)
    def _():
        wb_ref[...] = wt_ref[...].astype(BF16)

    acc = _dot_nt(x_ref[...], wb_ref[...])
    if act == "sigmoid":
        acc = jax.nn.sigmoid(acc)
    o_ref[...] = acc.astype(o_ref.dtype)


def matmul_window(x, w_t, layer, first_col, n_cols, *, act=None, out_dtype=F32):
    m, k = x.shape
    n_all = w_t.shape[1]
    first_row = layer * n_all + first_col
    assert first_row % SUBLANES == 0 and n_cols % COL_TILE == 0
    return pl.pallas_call(
        functools.partial(_mm_window_kernel, act=act), grid=(n_cols // COL_TILE, m // TM),
        in_specs=[pl.BlockSpec((TM, k), lambda j, i: (i, 0)),
                  pl.BlockSpec((pl.Element(COL_TILE), pl.Element(k)),
                               lambda j, i: (pl.multiple_of(first_row + j * COL_TILE, SUBLANES), 0))],
        out_specs=pl.BlockSpec((TM, COL_TILE), lambda j, i: (i, j)), scratch_shapes=[pltpu.VMEM((COL_TILE, k), BF16)],
        out_shape=jax.ShapeDtypeStruct((m, n_cols), out_dtype), compiler_params=_cp(2), name="matmul_window")(
            x, w_t.reshape(-1, k))


def _ple_kernel(h_ref, p_ref, wg_ref, wp_ref, o_ref, wgb_ref):
    gate = jax.nn.sigmoid(_dot(h_ref[...], _resident_bf16(wg_ref, wgb_ref)))
    o_ref[...] = gate * _dot(p_ref[...], wp_ref[...].astype(BF16))


def ple_delta(h, ple, w_gate, w_pe, layer, *, tn=512):
    m, k = h.shape
    n = w_gate.shape[-1]
    return pl.pallas_call(
        _ple_kernel, grid=(n // tn, m // TM),
        in_specs=[pl.BlockSpec((TM, k), lambda j, i: (i, 0)), pl.BlockSpec((TM, PLE_DIM), lambda j, i: (i, 0)),
                  _layer_block((k, tn), layer, lambda j, i: (0, j)),
                  _layer_block((PLE_DIM, tn), layer, lambda j, i: (0, j))],
        out_specs=pl.BlockSpec((TM, tn), lambda j, i: (i, j)), scratch_shapes=[pltpu.VMEM((k, tn), BF16)],
        out_shape=jax.ShapeDtypeStruct((m, n), F32), compiler_params=_cp(2), name="ple")(h, ple, w_gate, w_pe)


def _merge_kernel(*refs):
    o_refs, w_refs, g_refs, out_ref, wb_refs = refs[0:4], refs[4:8], refs[8:12], refs[12], refs[13:17]
    acc = None
    for o_ref, w_ref, g_ref, wb_ref in zip(o_refs, w_refs, g_refs, wb_refs):
        term = g_ref[...].astype(F32) * _dot(o_ref[...], _resident_bf16(w_ref, wb_ref))
        acc = term if acc is None else acc + term
    out_ref[...] = acc.astype(out_ref.dtype)


def merge(branches, weights, gates, layer, *, tn=512):
    m = branches[0].shape[0]
    nb = D_MODEL // tn
    in_specs = ([pl.BlockSpec((TM, W_MIX), lambda j, i: (i, 0))] * 4
                + [_layer_block((W_MIX, tn), layer, lambda j, i: (0, j))] * 4
                + [pl.BlockSpec((TM, tn), lambda j, i, b=b: (i, b * nb + j)) for b in range(4)])
    return pl.pallas_call(
        _merge_kernel, grid=(nb, m // TM), in_specs=in_specs,
        out_specs=pl.BlockSpec((TM, tn), lambda j, i: (i, j)), scratch_shapes=[pltpu.VMEM((W_MIX, tn), BF16)] * 4,
        out_shape=jax.ShapeDtypeStruct((m, D_MODEL), BF16), compiler_params=_cp(2), name="merge")(
            *branches, *weights, gates, gates, gates, gates)


CONV_HALO = 32


def _conv_kernel(val_ref, gate_ref, halo_ref, w_ref, cb_ref, lg_ref, lb_ref, o_ref, st_ref, aext, *, tt, rb, n_last):
    t = pl.program_id(1)

    @pl.when(t == 0)
    def _():
        aext[0:CONV_HALO, :] = halo_ref[...]

    aext[CONV_HALO:CONV_HALO + tt, :] = val_ref[...] * jax.nn.sigmoid(gate_ref[...])
    first = CONV_HALO - (CONV_WIDTH - 1)
    for r0 in range(0, tt, rb):
        acc = jnp.zeros((rb, W_MIX), F32) + cb_ref[...]
        for j in range(CONV_WIDTH):
            acc = acc + aext[r0 + first + j:r0 + first + j + rb, :] * w_ref[j:j + 1, :]
        y = _silu(_layernorm(acc, lg_ref[...], lb_ref[...]))
        o_ref[r0:r0 + rb, :] = y.astype(o_ref.dtype)

    @pl.when(t == pl.num_programs(1) - 1)
    def _():
        st_ref[...] = aext[n_last:n_last + CONV_HALO, :]

    if tt >= CONV_HALO:
        aext[0:CONV_HALO, :] = aext[tt:tt + CONV_HALO, :]


def conv_mixer(z, halo, conv_w, conv_b, ln_g, ln_b, *, nb, t_len, tt, rb, n_last):
    nt = t_len // tt
    w = jnp.zeros((CONV_HALO, W_MIX), F32).at[:CONV_WIDTH].set(conv_w)
    vec = pl.BlockSpec((1, W_MIX), lambda b, t: (0, 0))
    return pl.pallas_call(
        functools.partial(_conv_kernel, tt=tt, rb=rb, n_last=n_last), grid=(nb, nt),
        in_specs=[pl.BlockSpec((tt, W_MIX), lambda b, t: (b * nt + t, C_VAL // W_MIX)),
                  pl.BlockSpec((tt, W_MIX), lambda b, t: (b * nt + t, C_GATE // W_MIX)),
                  pl.BlockSpec((None, CONV_HALO, W_MIX), lambda b, t: (b, 0, 0)),
                  pl.BlockSpec((CONV_HALO, W_MIX), lambda b, t: (0, 0)), vec, vec, vec],
        out_specs=[pl.BlockSpec((tt, W_MIX), lambda b, t: (b * nt + t, 0)),
                   pl.BlockSpec((None, CONV_HALO, W_MIX), lambda b, t: (b, 0, 0))],
        out_shape=[jax.ShapeDtypeStruct((nb * t_len, W_MIX), BF16), jax.ShapeDtypeStruct((nb, CONV_HALO, W_MIX), F32)],
        scratch_shapes=[pltpu.VMEM((CONV_HALO + tt, W_MIX), F32)],
        compiler_params=_cp(2), name="conv")(z, z, halo, w, conv_b.reshape(1, -1), ln_g.reshape(1, -1), ln_b.reshape(1, -1))


def _sgu_kernel(u_ref, v_ref, wc_ref, bs_ref, lg_ref, lb_ref, o_ref, *, tt):
    for c0 in range(0, tt, SGU_CHUNK):
        rows = slice(c0, c0 + SGU_CHUNK)
        u = jax.nn.gelu(u_ref[rows, :])
        v = _layernorm(jax.nn.gelu(v_ref[rows, :]), lg_ref[...], lb_ref[...]).astype(BF16)
        parts = [_dot(wc_ref[g], v[:, g * LANES:(g + 1) * LANES]) for g in range(SGU_GROUPS)]
        mix = jnp.concatenate(parts, axis=1) + bs_ref[...]
        o_ref[rows, :] = (u * mix).astype(o_ref.dtype)


def sgu_mixer(z, w_causal, bias_rows, ln_g, ln_b, *, rows, tt=256):
    vec = pl.BlockSpec((1, W_MIX), lambda i: (0, 0))
    return pl.pallas_call(
        functools.partial(_sgu_kernel, tt=tt), grid=(rows // tt,),
        in_specs=[pl.BlockSpec((tt, W_MIX), lambda i: (i, C_U // W_MIX)),
                  pl.BlockSpec((tt, W_MIX), lambda i: (i, C_V // W_MIX)),
                  pl.BlockSpec((SGU_GROUPS, SGU_CHUNK, SGU_CHUNK), lambda i: (0, 0, 0)),
                  pl.BlockSpec((SGU_CHUNK, W_MIX), lambda i: (0, 0)), vec, vec],
        out_specs=pl.BlockSpec((tt, W_MIX), lambda i: (i, 0)),
        out_shape=jax.ShapeDtypeStruct((rows, W_MIX), BF16), compiler_params=_cp(1), name="sgu")(
            z, z, w_causal, bias_rows, ln_g.reshape(1, -1), ln_b.reshape(1, -1))


def _sgu_step_kernel(u_ref, v_ref, w0_ref, b0_ref, lg_ref, lb_ref, o_ref, vo_ref):
    u = jax.nn.gelu(u_ref[...])
    v = _layernorm(jax.nn.gelu(v_ref[...]), lg_ref[...], lb_ref[...])
    vo_ref[...] = v
    o_ref[...] = (u * (v * w0_ref[...] + b0_ref[...])).astype(o_ref.dtype)


def sgu_step(zs, w0, b0, ln_g, ln_b):
    n = zs.shape[0]
    vec = pl.BlockSpec((1, W_MIX), lambda i: (0, 0))
    return pl.pallas_call(
        _sgu_step_kernel, grid=(1,),
        in_specs=[pl.BlockSpec((n, W_MIX), lambda i: (0, C_U // W_MIX)),
                  pl.BlockSpec((n, W_MIX), lambda i: (0, C_V // W_MIX)), vec, vec, vec, vec],
        out_specs=[pl.BlockSpec((n, W_MIX), lambda i: (0, 0))] * 2,
        out_shape=[jax.ShapeDtypeStruct((n, W_MIX), BF16), jax.ShapeDtypeStruct((n, W_MIX), F32)],
        compiler_params=_cp(1), name="sgu_step")(zs, zs, w0, b0, ln_g.reshape(1, -1), ln_b.reshape(1, -1))


def _log_sigmoid(x):
    return jnp.minimum(x, 0.0) - jnp.log1p(jnp.exp(-jnp.abs(x)))


GLA_PAIRS = GLA_HEADS // 2
GLA_KEYS = GLA_HEADS * GLA_DK


def _gla_kernel(q_ref, k_ref, v_ref, r_ref, da_ref, wa_ref, ba_ref, gn_ref, s0_ref, o_ref, so_ref, st, *, tt, chunk, n_valid):
    t = pl.program_id(1)

    @pl.when(t == 0)
    def _():
        st[...] = s0_ref[...]

    row = lax.broadcasted_iota(jnp.int32, (chunk, chunk), 0)
    col = lax.broadcasted_iota(jnp.int32, (chunk, chunk), 1)
    causal = row >= col
    tril = causal.astype(BF16)
    low_half = lax.broadcasted_iota(jnp.int32, (chunk, LANES), 1) < GLA_DK
    for c0 in range(0, tt, chunk):
        rows = slice(c0, c0 + chunk)
        la = _log_sigmoid(_dot(da_ref[rows, :].astype(BF16), wa_ref[...]) + ba_ref[...]) * (1.0 / GLA_TAU)
        if n_valid < chunk:
            la = jnp.where(lax.broadcasted_iota(jnp.int32, la.shape, 0) < n_valid, la, 0.0)
        b = _split_dot_left(tril, la)
        bl = b[chunk - 1:chunk, :]
        kk = k_ref[rows, :]
        q_in = q_ref[rows, :] * (GLA_DK ** -0.5) * jnp.exp(b)
        k_in = (kk * jnp.exp(-b)).astype(BF16)
        k_end = kk * jnp.exp(bl - b)
        decay = jnp.exp(bl)
        vv = v_ref[rows, :].astype(BF16)
        for p in range(GLA_PAIRS):
            ps = slice(p * LANES, (p + 1) * LANES)
            state = st[p]
            state_b = state.astype(BF16)
            update = state * decay[:, ps]
            for e in range(2):
                hs = slice((2 * p + e) * GLA_DV, (2 * p + e + 1) * GLA_DV)
                mine = low_half if e == 0 else jnp.logical_not(low_half)
                q_h = jnp.where(mine, q_in[:, ps], 0.0).astype(BF16)
                k_h = jnp.where(mine, k_end[:, ps], 0.0).astype(BF16)
                a = jnp.where(causal, _dot_nt(q_h, k_in[:, ps]), 0.0).astype(BF16)
                o = _dot(a, vv[:, hs]) + _dot_nt(q_h, state_b)
                update = update + _dot_tn(vv[:, hs], k_h)
                y = o * lax.rsqrt(jnp.mean(o * o, axis=-1, keepdims=True) + EPS) * gn_ref[...]
                o_ref[rows, hs] = (y * _silu(r_ref[rows, hs])).astype(o_ref.dtype)
            st[p] = update

    @pl.when(t == pl.num_programs(1) - 1)
    def _():
        so_ref[...] = st[...]


def _split_dot_left(w, x):
    hi = x.astype(BF16)
    lo = (x - hi.astype(F32)).astype(BF16)
    return _dot(w, hi) + _dot(w, lo)


def gla_state_in(s):
    nb = s.shape[0]
    s = s.reshape(nb, GLA_PAIRS, 2, GLA_DK, GLA_DV)
    return jnp.transpose(s, (0, 1, 4, 2, 3)).reshape(nb, GLA_PAIRS, GLA_DV, 2 * GLA_DK)


def gla_state_out(s):
    nb = s.shape[0]
    s = s.reshape(nb, GLA_PAIRS, GLA_DV, 2, GLA_DK)
    return jnp.transpose(s, (0, 1, 3, 4, 2)).reshape(nb, GLA_HEADS, GLA_DK, GLA_DV)


def gla_mixer(z2, z3, s0_t, w_a2p, b_a, g_norm, *, nb, t_len, tt, chunk, n_valid):
    nt = t_len // tt
    rows = lambda width, c: pl.BlockSpec((tt, width), lambda b, t: (b * nt + t, c // width))
    state = pl.BlockSpec((None, GLA_PAIRS, GLA_DV, LANES), lambda b, t: (b, 0, 0, 0))
    return pl.pallas_call(
        functools.partial(_gla_kernel, tt=tt, chunk=chunk, n_valid=n_valid), grid=(nb, nt),
        in_specs=[rows(GLA_KEYS, C2_DQ), rows(GLA_KEYS, C2_DK), rows(W_MIX, C2_DV), rows(W_MIX, 0), rows(LANES, C2_DA),
                  pl.BlockSpec((LANES, GLA_KEYS), lambda b, t: (0, 0)),
                  pl.BlockSpec((1, GLA_KEYS), lambda b, t: (0, 0)),
                  pl.BlockSpec((1, LANES), lambda b, t: (0, 0)), state],
        out_specs=[pl.BlockSpec((tt, W_MIX), lambda b, t: (b * nt + t, 0)), state],
        out_shape=[jax.ShapeDtypeStruct((nb * t_len, W_MIX), BF16),
                   jax.ShapeDtypeStruct((nb, GLA_PAIRS, GLA_DV, LANES), F32)],
        scratch_shapes=[pltpu.VMEM((GLA_PAIRS, GLA_DV, LANES), F32)],
        compiler_params=_cp(2), name="gla")(z2, z2, z2, z3, z2, w_a2p, b_a.reshape(1, -1), g_norm.reshape(1, -1), s0_t)


SEG_PER_STEP = 128
CMP_ROWS = SEG_PER_STEP * CMP_STRIDE


def _compress_kernel(*refs, paged):
    if paged:
        n_in = PAGES_PER_STEP + 1
        in_refs = refs[1:1 + n_in]
        w_ref, o_ref = refs[1 + n_in], refs[2 + n_in]
        xs = refs[3 + n_in:]
    else:
        x_ref, w_ref, o_ref = refs[0:3]
        xs = refs[3:]
    for ch in range(4):
        plane, head = divmod(ch, KVH)
        cols = slice(ch * LANES, (ch + 1) * LANES)
        if paged:
            for j in range(PAGES_PER_STEP):
                xs[ch][j * PAGE_SIZE:(j + 1) * PAGE_SIZE, :] = in_refs[j][pl.ds(ch, PAGE_SIZE, stride=4), :]
            xs[ch][CMP_ROWS:CMP_ROWS + CMP_STRIDE, :] = in_refs[PAGES_PER_STEP][pl.ds(ch, CMP_STRIDE, stride=4), :]
        else:
            xs[ch][0:CMP_ROWS, :] = x_ref[:, cols]
            xs[ch][CMP_ROWS:CMP_ROWS + CMP_STRIDE, :] = jnp.zeros((CMP_STRIDE, LANES), F32)
        acc = jnp.zeros((SEG_PER_STEP, LANES), F32)
        for r in range(CMP_LEN):
            x = xs[ch][pl.ds(r, SEG_PER_STEP, stride=CMP_STRIDE), :]
            acc = acc + _dot(x.astype(BF16), w_ref[plane, r])
        o_ref[:, cols] = acc


def compress_prompt(z, w_cmp):
    return pl.pallas_call(
        functools.partial(_compress_kernel, paged=False), grid=(BATCH,),
        in_specs=[pl.BlockSpec((CMP_ROWS, 4 * LANES), lambda b: (b, C_CMP // (4 * LANES))),
                  pl.BlockSpec((2, CMP_LEN, LANES, LANES), lambda b: (0, 0, 0, 0))],
        out_specs=pl.BlockSpec((None, SEG_PER_STEP, 4 * LANES), lambda b: (b, 0, 0)),
        out_shape=jax.ShapeDtypeStruct((BATCH, SEG_PER_STEP, 4 * LANES), F32),
        scratch_shapes=[pltpu.VMEM((CMP_ROWS + CMP_STRIDE, LANES), F32)] * 4,
        compiler_params=_cp(1), name="compress_prompt")(z, w_cmp)


PAGES_PER_STEP = CMP_ROWS // PAGE_SIZE
N_PAGES = PAST_LEN // PAGE_SIZE


def _cache_rows(cache):
    return cache.reshape(cache.shape[0], cache.shape[1], 4 * PAGE_SIZE, HEAD_DIM)


def _page_spec(layer, page_of):
    return pl.BlockSpec((None, None, 4 * PAGE_SIZE, HEAD_DIM), lambda b, c, pt: (layer, page_of(b, c, pt), 0, 0))


def compress_paged(cache, page_table, w_cmp, layer):
    n_steps = N_PAGES // PAGES_PER_STEP
    specs = [_page_spec(layer, lambda b, c, pt, j=j: pt[b, jnp.minimum(c * PAGES_PER_STEP + j, N_PAGES - 1)])
             for j in range(PAGES_PER_STEP + 1)]
    grid_spec = pltpu.PrefetchScalarGridSpec(
        num_scalar_prefetch=1, grid=(DEC_BATCH, n_steps),
        in_specs=specs + [pl.BlockSpec((2, CMP_LEN, LANES, LANES), lambda b, c, pt: (0, 0, 0, 0))],
        out_specs=pl.BlockSpec((None, SEG_PER_STEP, 4 * LANES), lambda b, c, pt: (b, c, 0)),
        scratch_shapes=[pltpu.VMEM((CMP_ROWS + CMP_STRIDE, LANES), F32)] * 4)
    return pl.pallas_call(
        functools.partial(_compress_kernel, paged=True), grid_spec=grid_spec,
        out_shape=jax.ShapeDtypeStruct((DEC_BATCH, n_steps * SEG_PER_STEP, 4 * LANES), F32),
        compiler_params=_cp(2), name="compress_paged")(page_table, *([cache] * len(specs)), w_cmp)


def _top_select(score, n_lanes):
    lane = lax.broadcasted_iota(jnp.int32, score.shape, 1).astype(F32)
    sel = jnp.zeros(score.shape, F32)
    for _ in range(N_SELECT):
        m = jnp.max(score, axis=-1, keepdims=True)
        idx = jnp.min(jnp.where(score == m, lane, float(n_lanes)), axis=-1, keepdims=True)
        pick = lane == idx
        sel = jnp.where(pick, 1.0, sel)
        score = jnp.where(pick, -3e38, score)
    return sel


def _bias_tiles(delta_of_tile, n_tiles, t0, t1, t2):
    tiles = []
    for j in range(n_tiles):
        delta = delta_of_tile(j)
        tiles.append(jnp.where(delta == 0, t0, jnp.where(delta == 1, t1, t2)))
    return jnp.concatenate(tiles, axis=1)


WIN_BAND = WINDOW + Q_BLOCK


SLC_KEY_STEP = 512


def _nsa_prompt_kernel(q_ref, cbk_ref, cbv_ref, ks_ref, vs_ref, kw_ref, vw_ref, g_ref, bc_ref, t0_ref, t1_ref, t2_ref,
                       covt_ref, e_ref, o_ref, oslc_ref):
    h = pl.program_id(1)
    i = pl.program_id(2)
    q = q_ref[...] * (HEAD_DIM ** -0.5)
    qs = jnp.concatenate([q[:, g * LANES:(g + 1) * LANES] for g in range(GRP)], axis=0).astype(BF16)
    qpos = i * Q_BLOCK + lax.broadcasted_iota(jnp.int32, (Q_BLOCK, 1), 0)
    lane = lax.broadcasted_iota(jnp.int32, (Q_BLOCK, LANES), 1)

    sc = _dot_nt(qs, cbk_ref[...].astype(BF16))
    cmask = (lane * CMP_STRIDE + (CMP_LEN - 1) <= qpos) & (lane < SEQ // CMP_STRIDE - 1)
    p_c = [_masked_softmax(sc[g * Q_BLOCK:(g + 1) * Q_BLOCK] + bc_ref[g], cmask) for g in range(GRP)]
    o_cmp = _dot(jnp.concatenate(p_c, axis=0).astype(BF16), cbv_ref[...].astype(BF16))
    psum = p_c[0] + p_c[1] + p_c[2] + p_c[3]
    p_hi = psum.astype(BF16)
    p_lo = (psum - p_hi.astype(F32)).astype(BF16)
    imp_t = _dot_nt(covt_ref[...], p_hi) + _dot_nt(covt_ref[...], p_lo)

    n_blk = SEQ // SLC_LEN
    blk = lax.broadcasted_iota(jnp.int32, (n_blk, Q_BLOCK), 0)
    qpos_l = i * Q_BLOCK + lax.broadcasted_iota(jnp.int32, (n_blk, Q_BLOCK), 1)
    cur = lax.shift_right_logical(qpos_l, 6)
    valid = blk * SLC_LEN <= qpos_l
    forced = (blk == 0) | (blk == cur) | (blk == cur - 1)
    score = jnp.where(valid, imp_t[0:n_blk] + jnp.where(forced, FORCE_BONUS, 0.0), NEG)
    rank = jnp.zeros((n_blk, Q_BLOCK), F32)
    for other_blk in range(n_blk):
        other = score[other_blk:other_blk + 1, :]
        rank = rank + jnp.where((other > score) | ((other == score) & (blk > other_blk)), 1.0, 0.0)
    sel_t = jnp.where(valid & (rank < N_SELECT), 1.0, 0.0).astype(BF16)

    def selected(n_keys):
        kpos = lax.broadcasted_iota(jnp.int32, (Q_BLOCK, n_keys), 1)
        smask = (_dot_tn(sel_t, e_ref[0:n_blk, 0:n_keys]) > 0.5) & (kpos <= qpos)
        ss = _dot_nt(qs, ks_ref[0:n_keys, :].astype(BF16))
        p_s = []
        for g in range(GRP):
            bias = _bias_tiles(lambda j: i - j, n_keys // Q_BLOCK, t0_ref[g], t1_ref[g], t2_ref[g])
            p_s.append(_masked_softmax(ss[g * Q_BLOCK:(g + 1) * Q_BLOCK] + bias, smask))
        oslc_ref[...] = _dot(jnp.concatenate(p_s, axis=0).astype(BF16), vs_ref[0:n_keys, :].astype(BF16))

    tiles_per_step = SLC_KEY_STEP // Q_BLOCK
    for hi in range(tiles_per_step, SEQ // Q_BLOCK + 1, tiles_per_step):
        pl.when((i >= hi - tiles_per_step) & (i < hi))(functools.partial(selected, hi * Q_BLOCK))
    o_slc = oslc_ref[...]

    first_tile = jnp.maximum(i - WINDOW // Q_BLOCK, 0)
    start = pl.multiple_of(first_tile * Q_BLOCK, Q_BLOCK)
    dist = qpos - (start + lax.broadcasted_iota(jnp.int32, (Q_BLOCK, WIN_BAND), 1))
    wmask = (dist >= 0) & (dist <= WINDOW)
    sw = _dot_nt(qs, kw_ref[pl.ds(start, WIN_BAND), :].astype(BF16))
    p_w = []
    for g in range(GRP):
        bias = _bias_tiles(lambda j: i - first_tile - j, WIN_BAND // Q_BLOCK, t0_ref[g], t1_ref[g], t2_ref[g])
        p_w.append(_masked_softmax(sw[g * Q_BLOCK:(g + 1) * Q_BLOCK] + bias, wmask))
    o_win = _dot(jnp.concatenate(p_w, axis=0).astype(BF16), vw_ref[pl.ds(start, WIN_BAND), :].astype(BF16))

    gates = jax.nn.sigmoid(g_ref[...])
    for g in range(GRP):
        rows = slice(g * Q_BLOCK, (g + 1) * Q_BLOCK)
        base = (h * GRP + g) * 3
        gc = [jnp.sum(jnp.where(lane == base + br, gates, 0.0), axis=-1, keepdims=True) for br in range(3)]
        o = gc[0] * o_cmp[rows] + gc[1] * o_slc[rows] + gc[2] * o_win[rows]
        o_ref[:, g * LANES:(g + 1) * LANES] = o.astype(o_ref.dtype)


def nsa_prompt(z, cb, bias_c, t0, t1, t2, cover_t, expand):
    nq = SEQ // Q_BLOCK
    kv = lambda c: pl.BlockSpec((SEQ, LANES), lambda b, h, i, c=c: (b, c // LANES + h))
    cbs = lambda c: pl.BlockSpec((None, SEG_PER_STEP, LANES), lambda b, h, i, c=c: (b, 0, c + h))
    toe = pl.BlockSpec((GRP, Q_BLOCK, LANES), lambda b, h, i: (h, 0, 0))
    return pl.pallas_call(
        _nsa_prompt_kernel, grid=(BATCH, KVH, nq),
        in_specs=[pl.BlockSpec((Q_BLOCK, GRP * LANES), lambda b, h, i: (b * nq + i, C_Q // (GRP * LANES) + h)),
                  cbs(0), cbs(KVH), kv(C_SLC), kv(C_SLC + KVH * LANES), kv(C_WIN), kv(C_WIN + KVH * LANES),
                  pl.BlockSpec((Q_BLOCK, LANES), lambda b, h, i: (b * nq + i, C_CG // LANES)),
                  pl.BlockSpec((GRP, Q_BLOCK, LANES), lambda b, h, i: (h, i, 0)), toe, toe, toe,
                  pl.BlockSpec((LANES, LANES), lambda b, h, i: (0, 0)),
                  pl.BlockSpec((LANES, SEQ), lambda b, h, i: (0, 0))],
        out_specs=pl.BlockSpec((Q_BLOCK, GRP * LANES), lambda b, h, i: (b * nq + i, h)),
        scratch_shapes=[pltpu.VMEM((GRP * Q_BLOCK, LANES), F32)],
        out_shape=jax.ShapeDtypeStruct((M_PROMPT, W_MIX), BF16), compiler_params=_cp(3), name="nsa_prompt")(
            z, cb, cb, z, z, z, z, z, bias_c, t0, t1, t2, cover_t, expand)


N_CMP_S = PAST_LEN // CMP_STRIDE
N_SLC_S = 384


def _head_rows(shape):
    return lax.broadcasted_iota(jnp.int32, shape, 0) < GRP


def _nsa_step_a_kernel(q_ref, cb_ref, bc_ref, cov_ref, wb_ref, kvn_ref, bw_ref, bwn_ref, ocw_ref, sel_ref):
    q = q_ref[...] * (HEAD_DIM ** -0.5)
    qb = q.astype(BF16)
    cb = cb_ref[...]
    first = _head_rows((NSA_HEADS, N_CMP_S))
    s = jnp.where(first, _dot_nt(qb, cb[:, 0:LANES].astype(BF16)), _dot_nt(qb, cb[:, LANES:2 * LANES].astype(BF16)))
    kidx = lax.broadcasted_iota(jnp.int32, (NSA_HEADS, N_CMP_S), 1)
    p = _masked_softmax(s + bc_ref[...], kidx * CMP_STRIDE + (CMP_LEN - 1) <= PAST_LEN)
    pb = p.astype(BF16)
    o_cmp = jnp.where(_head_rows((NSA_HEADS, LANES)), _dot(pb, cb[:, 2 * LANES:3 * LANES].astype(BF16)),
                      _dot(pb, cb[:, 3 * LANES:4 * LANES].astype(BF16)))
    row = lax.broadcasted_iota(jnp.int32, (NSA_HEADS, N_CMP_S), 0)
    ps0 = jnp.sum(jnp.where(first, p, 0.0), axis=0, keepdims=True)
    ps1 = jnp.sum(jnp.where(first, 0.0, p), axis=0, keepdims=True)
    psum = jnp.where(row == 0, ps0, jnp.where(row == 1, ps1, 0.0))
    imp = _split_dot(psum, cov_ref[...])
    lane = lax.broadcasted_iota(jnp.int32, (NSA_HEADS, N_SLC_S), 1)
    cur = PAST_LEN // SLC_LEN
    valid = lane <= cur
    forced = (lane == 0) | (lane == cur) | (lane == cur - 1)
    score = jnp.where(valid, imp + jnp.where(forced, FORCE_BONUS, 0.0), NEG)
    sel_ref[...] = _top_select(score, N_SLC_S) * valid.astype(F32)

    wb = wb_ref[...]
    firstw = _head_rows((NSA_HEADS, WINDOW))
    sw = jnp.where(firstw, _dot_nt(qb, wb[:, 0:LANES].astype(BF16)), _dot_nt(qb, wb[:, LANES:2 * LANES].astype(BF16)))
    sw = sw + bw_ref[...]
    firstl = _head_rows((NSA_HEADS, LANES))
    kvn = kvn_ref[...]
    k_new = jnp.where(firstl, kvn[:, 0:LANES], kvn[:, LANES:2 * LANES])
    v_new = jnp.where(firstl, kvn[:, 2 * LANES:3 * LANES], kvn[:, 3 * LANES:4 * LANES])
    s_new = jnp.sum(q * k_new, axis=-1, keepdims=True) + bwn_ref[:, 0:1]
    m = jnp.maximum(jnp.max(sw, axis=-1, keepdims=True), s_new)
    pw = jnp.exp(sw - m)
    pn = jnp.exp(s_new - m)
    pwb = pw.astype(BF16)
    acc = jnp.where(firstl, _dot(pwb, wb[:, 2 * LANES:3 * LANES].astype(BF16)), _dot(pwb, wb[:, 3 * LANES:4 * LANES].astype(BF16)))
    o_win = (acc + pn * v_new) / (jnp.sum(pw, axis=-1, keepdims=True) + pn)
    ocw_ref[:, 0:LANES] = o_cmp
    ocw_ref[:, LANES:2 * LANES] = o_win


def nsa_step_a(q_s, cb_s, bias_c, cover, win_buf, kvw_new, bias_w, bias_wn):
    per_b = lambda *shape: pl.BlockSpec((None,) + shape, lambda b: (b,) + (0,) * len(shape))
    whole = lambda *shape: pl.BlockSpec(shape, lambda b: (0,) * len(shape))
    return pl.pallas_call(
        _nsa_step_a_kernel, grid=(DEC_BATCH,),
        in_specs=[per_b(NSA_HEADS, LANES), per_b(N_CMP_S, 4 * LANES), whole(NSA_HEADS, N_CMP_S), whole(N_CMP_S, N_SLC_S),
                  per_b(WINDOW, 4 * LANES), per_b(1, 4 * LANES), whole(NSA_HEADS, WINDOW), whole(NSA_HEADS, LANES)],
        out_specs=[per_b(NSA_HEADS, 2 * LANES), per_b(NSA_HEADS, N_SLC_S)],
        out_shape=[jax.ShapeDtypeStruct((DEC_BATCH, NSA_HEADS, 2 * LANES), F32),
                   jax.ShapeDtypeStruct((DEC_BATCH, NSA_HEADS, N_SLC_S), F32)],
        compiler_params=_cp(1), name="nsa_step_a")(q_s, cb_s, bias_c, cover, win_buf, kvw_new, bias_w, bias_wn)


def _nsa_step_b_kernel(*refs):
    pages = refs[1:1 + PAGES_PER_STEP]
    (q_ref, sel_ref, e_ref, bs_ref, kvn_ref, seln_ref, bn_ref, ocw_ref, g_ref, o_ref, m_sc, l_sc, acc_sc) = refs[1 + PAGES_PER_STEP:]
    c = pl.program_id(1)

    @pl.when(c == 0)
    def _():
        m_sc[...] = jnp.full(m_sc.shape, NEG, F32)
        l_sc[...] = jnp.zeros(l_sc.shape, F32)
        acc_sc[...] = jnp.zeros(acc_sc.shape, F32)

    q = q_ref[...] * (HEAD_DIM ** -0.5)
    qb = q.astype(BF16)
    kv = [jnp.concatenate([pg[pl.ds(ch, PAGE_SIZE, stride=4), :].astype(BF16) for pg in pages], axis=0)
          for ch in range(4)]
    first = _head_rows((NSA_HEADS, CMP_ROWS))
    firstl = _head_rows((NSA_HEADS, LANES))
    s = jnp.where(first, _dot_nt(qb, kv[0]), _dot_nt(qb, kv[1])) + bs_ref[...]
    mask = _dot(sel_ref[...].astype(BF16), e_ref[...]) > 0.5
    s = jnp.where(mask, s, NEG)
    m_old = m_sc[...]
    m_new = jnp.maximum(m_old, jnp.max(s, axis=-1, keepdims=True))
    alpha = jnp.exp(m_old - m_new)
    p = jnp.where(mask, jnp.exp(s - m_new), 0.0)
    pb = p.astype(BF16)
    l_sc[...] = alpha * l_sc[...] + jnp.sum(p, axis=-1, keepdims=True)
    acc_sc[...] = alpha * acc_sc[...] + jnp.where(firstl, _dot(pb, kv[2]), _dot(pb, kv[3]))
    m_sc[...] = m_new

    @pl.when(c == pl.num_programs(1) - 1)
    def _():
        kvn = kvn_ref[...]
        k_new = jnp.where(firstl, kvn[:, 0:LANES], kvn[:, LANES:2 * LANES])
        v_new = jnp.where(firstl, kvn[:, 2 * LANES:3 * LANES], kvn[:, 3 * LANES:4 * LANES])
        on = seln_ref[:, 0:1] > 0.5
        s_n = jnp.where(on, jnp.sum(q * k_new, axis=-1, keepdims=True) + bn_ref[:, 0:1], NEG)
        m_old = m_sc[...]
        m_fin = jnp.maximum(m_old, s_n)
        alpha = jnp.exp(m_old - m_fin)
        p_n = jnp.where(on, jnp.exp(s_n - m_fin), 0.0)
        l_fin = alpha * l_sc[...] + p_n
        o_slc = (alpha * acc_sc[...] + p_n * v_new) / jnp.maximum(l_fin, 1e-30)
        gates = jax.nn.sigmoid(g_ref[...])
        o = gates[:, 0:1] * ocw_ref[:, 0:LANES] + gates[:, 1:2] * o_slc + gates[:, 2:3] * ocw_ref[:, LANES:2 * LANES]
        o_ref[...] = o.astype(o_ref.dtype)


def nsa_step_b(cache, page_table, layer, q_s, sel_r, expand, bias_s, kvs_new, sel_new, bias_n, ocw, gates):
    n_steps = N_PAGES // PAGES_PER_STEP
    page_specs = [_page_spec(layer, lambda b, c, pt, j=j: pt[b, c * PAGES_PER_STEP + j]) for j in range(PAGES_PER_STEP)]
    per_b = lambda *shape: pl.BlockSpec((None,) + shape, lambda b, c, pt: (b,) + (0,) * len(shape))
    whole = lambda *shape: pl.BlockSpec(shape, lambda b, c, pt: (0,) * len(shape))
    grid_spec = pltpu.PrefetchScalarGridSpec(
        num_scalar_prefetch=1, grid=(DEC_BATCH, n_steps),
        in_specs=page_specs + [
            per_b(NSA_HEADS, LANES),
            pl.BlockSpec((None, None, NSA_HEADS, LANES), lambda b, c, pt: (b, c, 0, 0)),
            whole(LANES, CMP_ROWS),
            pl.BlockSpec((NSA_HEADS, CMP_ROWS), lambda b, c, pt: (0, c)),
            per_b(1, 4 * LANES), per_b(NSA_HEADS, LANES), whole(NSA_HEADS, LANES),
            per_b(NSA_HEADS, 2 * LANES), per_b(NSA_HEADS, LANES)],
        out_specs=per_b(NSA_HEADS, LANES),
        scratch_shapes=[pltpu.VMEM((NSA_HEADS, 1), F32), pltpu.VMEM((NSA_HEADS, 1), F32), pltpu.VMEM((NSA_HEADS, LANES), F32)])
    return pl.pallas_call(
        _nsa_step_b_kernel, grid_spec=grid_spec,
        out_shape=jax.ShapeDtypeStruct((DEC_BATCH, NSA_HEADS, LANES), BF16),
        compiler_params=_cp(2), name="nsa_step_b")(
            page_table, *([cache] * len(page_specs)), q_s, sel_r, expand, bias_s, kvs_new, sel_new, bias_n, ocw, gates)


def _rel_bias_of(rel_bias, dist):
    n = jnp.maximum(dist, 0)
    max_exact = REL_BUCKETS // 2
    nf = jnp.maximum(n, 1).astype(F32)
    large = max_exact + (jnp.log(nf / max_exact) / math.log(REL_MAX_DIST / max_exact)
                         * (REL_BUCKETS - max_exact)).astype(jnp.int32)
    bucket = jnp.where(n < max_exact, n, jnp.minimum(large, REL_BUCKETS - 1))[None]
    out = jnp.zeros((NSA_HEADS,) + dist.shape, F32)
    for k in range(REL_BUCKETS):
        out = jnp.where(bucket == k, rel_bias[k].reshape((NSA_HEADS,) + (1,) * dist.ndim), out)
    return out


def _nsa_tables(rel_bias):
    bias_of = functools.partial(_rel_bias_of, rel_bias)
    r = jnp.arange(Q_BLOCK)
    diff = r[:, None] - r[None, :]
    t0 = bias_of(diff)
    t1 = bias_of(Q_BLOCK + diff)
    t2 = bias_of(2 * Q_BLOCK + diff)
    bias_c = bias_of(jnp.arange(SEQ)[:, None] - (jnp.arange(LANES)[None, :] * CMP_STRIDE + CMP_LEN - 1))
    k = jnp.arange(LANES)
    j = jnp.arange(LANES)
    cover = ((k[:, None] * CMP_STRIDE < (j[None, :] + 1) * SLC_LEN) & (k[:, None] * CMP_STRIDE + CMP_LEN > j[None, :] * SLC_LEN)
             & (k[:, None] < SEQ // CMP_STRIDE - 1) & (j[None, :] < SEQ // SLC_LEN)).astype(BF16)
    expand = (jnp.arange(SEQ)[None, :] // SLC_LEN == jnp.arange(LANES)[:, None]).astype(BF16)
    ks = jnp.arange(N_CMP_S)
    bias_cs = bias_of(PAST_LEN - (ks * CMP_STRIDE + CMP_LEN - 1))
    js = jnp.arange(N_SLC_S)
    cover_s = ((ks[:, None] * CMP_STRIDE < (js[None, :] + 1) * SLC_LEN) & (ks[:, None] * CMP_STRIDE + CMP_LEN > js[None, :] * SLC_LEN)
               & (ks[:, None] < N_CMP_S - 1) & (js[None, :] <= PAST_LEN // SLC_LEN)).astype(BF16)
    bias_w = bias_of(WINDOW - jnp.arange(WINDOW))
    bias_0 = bias_of(jnp.zeros((LANES,), jnp.int32))
    bias_s = bias_of(PAST_LEN - jnp.arange(PAST_LEN))
    return dict(t0=t0, t1=t1, t2=t2, bias_c=bias_c, cover_t=cover.T, expand=expand, bias_cs=bias_cs, cover_s=cover_s,
                bias_w=bias_w, bias_0=bias_0, bias_s=bias_s)


def _kv_export_kernel(c_ref, s_ref, w_ref, co_ref, so_ref, wo_ref):
    co_ref[...] = c_ref[...]
    so_ref[...] = s_ref[...]
    wo_ref[...] = w_ref[...]


def kv_export(z1):
    kv_w = 4 * LANES
    full = lambda c: pl.BlockSpec((SEQ, kv_w), lambda b: (b, c // kv_w))
    out_full = pl.BlockSpec((None, SEQ, kv_w), lambda b: (b, 0, 0))
    return pl.pallas_call(
        _kv_export_kernel, grid=(BATCH,),
        in_specs=[full(C_CMP), full(C_SLC),
                  pl.BlockSpec((WINDOW, kv_w), lambda b: ((b + 1) * (SEQ // WINDOW) - 1, C_WIN // kv_w))],
        out_specs=[out_full, out_full, pl.BlockSpec((None, WINDOW, kv_w), lambda b: (b, 0, 0))],
        out_shape=[jax.ShapeDtypeStruct((BATCH, SEQ, kv_w), F32)] * 2 + [jax.ShapeDtypeStruct((BATCH, WINDOW, kv_w), F32)],
        compiler_params=_cp(1), name="kv_export")(z1, z1, z1)


def _mixers(layer, z1, z2, z3, tabs, wts, st):
    new = {}
    sample = slice(M_PROMPT, M_PROMPT + DEC_BATCH)
    zs = z1[sample]

    def one_tile_per_batch(a):
        return jnp.zeros((DEC_BATCH, 8, a.shape[1]), F32).at[:, 0].set(a).reshape(DEC_BATCH * 8, a.shape[1])

    tail = jnp.zeros((M_SLAB - M_PROMPT - DEC_BATCH, W_MIX), BF16)
    slab = lambda p, s: jnp.concatenate([p, s, tail], axis=0)

    zero_halo = jnp.zeros((BATCH, CONV_HALO, W_MIX), F32)
    conv_args = (wts['conv_w'], wts['conv_b'], wts['conv_ln_g'], wts['conv_ln_b'])
    oa_p, cst_p = conv_mixer(z1, zero_halo, *conv_args, nb=BATCH, t_len=SEQ, tt=256, rb=32, n_last=256)
    halo_s = jnp.pad(st['conv'], ((0, 0), (CONV_HALO - (CONV_WIDTH - 1), 0), (0, 0)))
    oa_s, cst_s = conv_mixer(one_tile_per_batch(zs[:, :C_U]), halo_s, *conv_args, nb=DEC_BATCH, t_len=8, tt=8, rb=8, n_last=1)
    new['conv_p'] = cst_p[:, CONV_HALO - (CONV_WIDTH - 1):]
    new['conv_s'] = cst_s[:, CONV_HALO - (CONV_WIDTH - 1):]
    o_a = slab(oa_p, oa_s[::8])

    ws = wts['sgu_ws']
    w_causal = (ws * jnp.tril(jnp.ones((SGU_CHUNK, SGU_CHUNK), F32))).astype(BF16)
    bias_rows = jnp.repeat(wts['sgu_bs'].T, LANES, axis=1)
    ob_p = sgu_mixer(z1, w_causal, bias_rows, wts['sgu_ln_g'], wts['sgu_ln_b'], rows=M_PROMPT)
    w0 = jnp.repeat(ws[:, 0, 0], LANES)[None, :]
    ob_s, v_s = sgu_step(zs, w0, bias_rows[0:1], wts['sgu_ln_g'], wts['sgu_ln_b'])
    new['sgu_v'] = v_s[:, None, :]
    o_b = slab(ob_p, ob_s)

    w_a2p = jnp.zeros((LANES, GLA_KEYS), F32).at[:GLA_RANK].set(wts['gla_w_a2']).astype(BF16)
    gla_args = (w_a2p, wts['gla_b_a'], wts['gla_norm'])
    s0_p = jnp.zeros((BATCH, GLA_PAIRS, GLA_DV, LANES), F32)
    od_p, sp = gla_mixer(z2, z3, s0_p, *gla_args, nb=BATCH, t_len=SEQ, tt=256, chunk=GLA_CHUNK, n_valid=GLA_CHUNK)
    od_s, ss = gla_mixer(one_tile_per_batch(z2[sample]), one_tile_per_batch(z3[sample]), gla_state_in(st['gla']), *gla_args,
                         nb=DEC_BATCH, t_len=8, tt=8, chunk=8, n_valid=1)
    new['gla_p'] = gla_state_out(sp)
    new['gla_s'] = gla_state_out(ss)
    o_d = slab(od_p, od_s[::8])

    w_cmp = wts['nsa_w_cmp'].astype(BF16)
    cb_p = compress_prompt(z1, w_cmp)
    oc_p = nsa_prompt(z1, cb_p, tabs['bias_c'], tabs['t0'], tabs['t1'], tabs['t2'], tabs['cover_t'], tabs['expand'])
    pt = st['page_table']
    cb_s = compress_paged(_cache_rows(st['cmp']), pt, w_cmp, layer)
    q_s = zs[:, C_Q:C_Q + NSA_HEADS * LANES].reshape(DEC_BATCH, NSA_HEADS, LANES)
    kvw_new = zs[:, None, C_WIN:C_WIN + 4 * LANES]
    kvs_new = zs[:, None, C_SLC:C_SLC + 4 * LANES]
    win_buf = st['win'].reshape(DEC_BATCH, WINDOW, 4 * LANES)
    ocw, sel = nsa_step_a(q_s, cb_s, tabs['bias_cs'], tabs['cover_s'], win_buf, kvw_new, tabs['bias_w'], tabs['bias_0'])
    n_steps = N_PAGES // PAGES_PER_STEP
    blocks_per_step = CMP_ROWS // SLC_LEN
    sel_kv = jnp.repeat(sel[:, :KVH], GRP, axis=1)
    sel_r = sel_kv[:, :, :n_steps * blocks_per_step].reshape(DEC_BATCH, NSA_HEADS, n_steps, blocks_per_step)
    sel_r = jnp.pad(jnp.swapaxes(sel_r, 1, 2), ((0, 0), (0, 0), (0, 0), (0, LANES - blocks_per_step)))
    sel_new = jnp.broadcast_to(sel_kv[:, :, PAST_LEN // SLC_LEN, None], (DEC_BATCH, NSA_HEADS, LANES))
    gates_s = zs[:, C_CG:C_CG + 3 * NSA_HEADS].reshape(DEC_BATCH, NSA_HEADS, 3)
    gates_s = jnp.pad(gates_s, ((0, 0), (0, 0), (0, LANES - 3)))
    oc_s = nsa_step_b(_cache_rows(st['slc']), pt, layer, q_s, sel_r, tabs['expand'], tabs['bias_s'], kvs_new, sel_new, tabs['bias_0'], ocw, gates_s)
    o_c = slab(oc_p, oc_s.reshape(DEC_BATCH, W_MIX))

    cmp_p, slc_p, win_p = kv_export(z1)
    new['cmp_p'] = cmp_p.reshape(BATCH, SEQ // PAGE_SIZE, PAGE_SIZE, 2, KVH, HEAD_DIM)
    new['slc_p'] = slc_p.reshape(BATCH, SEQ // PAGE_SIZE, PAGE_SIZE, 2, KVH, HEAD_DIM)
    new['win_p'] = win_p.reshape(BATCH, WINDOW, 2, KVH, HEAD_DIM)
    kvs = lambda c: zs[:, c:c + 4 * LANES].reshape(DEC_BATCH, 1, 2, KVH, HEAD_DIM)
    new['cmp_s'] = kvs(C_CMP)
    new['slc_s'] = kvs(C_SLC)
    new['win_s'] = jnp.concatenate([st['win'][:, 1:], kvs(C_WIN)], axis=1)
    return (o_a, o_b, o_c, o_d), new


def kernel(x_prompt, x_sample, cache_cmp_kv, cache_slc_kv, state_win_kv, state_conv, state_gla, page_table, p_prompt, p_sample, ffn1_norm, ffn1_w_in, ffn1_w_out, mix_norm, w_in, conv_w, conv_b, conv_ln_g, conv_ln_b, conv_w_out, sgu_ln_g, sgu_ln_b, sgu_ws, sgu_bs, sgu_w_out, nsa_w_cmp, nsa_w_out, gla_w_a2, gla_b_a, gla_norm, gla_w_out, w_out, ffn2_norm, ffn2_w_in, ffn2_w_out, pe_norm, w_pe, w_pe_gate, rel_bias, final_norm):
    n_tail = M_SLAB - M_PROMPT - DEC_BATCH
    x = jnp.concatenate([x_prompt.reshape(M_PROMPT, D_MODEL), x_sample.reshape(DEC_BATCH, D_MODEL),
                         jnp.zeros((n_tail, D_MODEL), F32)], axis=0)
    tabs = _nsa_tables(rel_bias)
    w_in_t = jnp.swapaxes(w_in, 1, 2)
    delta = None
    news = []
    for i in range(DEPTH):
        x, h = add_norm(x, delta, ffn1_norm[i])
        delta = ffn_half(h, ffn1_w_in, ffn1_w_out, i)
        x, h = add_norm(x, delta, mix_norm[i])
        z1 = matmul_window(h, w_in_t, i, 0, Z1W)
        z2 = matmul_window(h, w_in_t, i, Z2_OFF, Z2W)
        z3 = matmul_window(h, w_in_t, i, Z3_OFF, Z3W)
        gates = matmul_window(h, w_in_t, i, G_OFF, N_BRANCH * D_MODEL, act="sigmoid", out_dtype=BF16)
        wts = dict(conv_w=conv_w[i], conv_b=conv_b[i], conv_ln_g=conv_ln_g[i], conv_ln_b=conv_ln_b[i],
                   sgu_ws=sgu_ws[i], sgu_bs=sgu_bs[i], sgu_ln_g=sgu_ln_g[i], sgu_ln_b=sgu_ln_b[i],
                   nsa_w_cmp=nsa_w_cmp[i], gla_w_a2=gla_w_a2[i], gla_b_a=gla_b_a[i], gla_norm=gla_norm[i])
        st = dict(cmp=cache_cmp_kv, slc=cache_slc_kv, win=state_win_kv[i], conv=state_conv[i], gla=state_gla[i],
                  page_table=page_table)
        branches, new = _mixers(i, z1, z2, z3, tabs, wts, st)
        news.append(new)
        merged = merge(branches, (conv_w_out, sgu_w_out, nsa_w_out, gla_w_out), gates, i)
        delta = matmul(merged, w_out, tn=1024, layer=i)
        x, h = add_norm(x, delta, ffn2_norm[i])
        delta = ffn_half(h, ffn2_w_in, ffn2_w_out, i)
        x, h = add_norm(x, delta, pe_norm[i])
        ple = jnp.concatenate([p_prompt[i].reshape(M_PROMPT, PLE_DIM), p_sample[i].reshape(DEC_BATCH, PLE_DIM),
                               jnp.zeros((n_tail, PLE_DIM), F32)], axis=0).astype(BF16)
        delta = ple_delta(h, ple, w_pe_gate, w_pe, i)
    y_prompt = final_norm_rows(x, delta, final_norm, first_row=0, n_rows=M_PROMPT, tile=256)
    y_sample = final_norm_rows(x, delta, final_norm, first_row=M_PROMPT, n_rows=16, tile=16)[:DEC_BATCH]
    stack = lambda name: jnp.stack([n[name] for n in news], axis=0)
    return (y_prompt.reshape(BATCH, SEQ, D_MODEL), y_sample.reshape(DEC_BATCH, 1, D_MODEL),
            stack('cmp_p'), stack('cmp_s'), stack('slc_p'), stack('slc_s'), stack('win_p'), stack('win_s'),
            stack('conv_p'), stack('conv_s'), stack('gla_p'), stack('gla_s'), stack('sgu_v'))
```

```python
import functools
import math

import jax
import jax.numpy as jnp
from jax import lax
from jax.experimental import pallas as pl
from jax.experimental.pallas import tpu as pltpu

F32 = jnp.float32
BF16 = jnp.bfloat16

D_MODEL = 4096
BATCH = 4
SEQ = 2048
DEPTH = 2
DEC_BATCH = 8
PAST_LEN = 16384
PAGE_SIZE = 128
D_FF = 11008
W_MIX = 1024
N_BRANCH = 4
CONV_WIDTH = 31
SGU_CHUNK = 128
SGU_GROUPS = 8
HEAD_DIM = 128
NSA_HEADS = 8
KVH = 2
GRP = 4
CMP_STRIDE = 16
CMP_LEN = 32
SLC_LEN = 64
N_SELECT = 16
WINDOW = 512
Q_BLOCK = 128
FORCE_BONUS = 1.0e4
GLA_HEADS = 8
GLA_DV = 128
GLA_DK = 64
GLA_RANK = 16
GLA_TAU = 16.0
GLA_CHUNK = 64
REL_BUCKETS = 32
REL_MAX_DIST = 128
PLE_DIM = 256
EPS = 1e-6

LANES = 128
VMEM_LIMIT_BYTES = 56 * 1024 * 1024
M_PROMPT = BATCH * SEQ
M_SLAB = M_PROMPT + 16
TM = 912
TM_NORM = 304
FFN_CHUNK = 256
FFN_OUT_SLAB = 512
NEG = -1e30

C_VAL, C_GATE, C_U, C_V, C_Q, C_CMP, C_SLC, C_WIN, C_CG = 0, 1024, 2048, 3072, 4096, 5120, 5632, 6144, 6656
Z1W = 7168
Z2_OFF, Z2W = 6680, 2560
C2_DQ, C2_DK, C2_DV, C2_DA = 0, 512, 1024, 2048
Z3_OFF, Z3W = 8744, 1024
G_OFF = 9768

NT_DIMS = (((1,), (1,)), ((), ()))
TN_DIMS = (((0,), (0,)), ((), ()))


def _cp(n_axes, vmem=VMEM_LIMIT_BYTES):
    return pltpu.CompilerParams(dimension_semantics=("arbitrary",) * n_axes, vmem_limit_bytes=vmem)


def _dot(a, b):
    return jnp.dot(a, b, preferred_element_type=F32)


def _dot_nt(a, b):
    return lax.dot_general(a, b, NT_DIMS, preferred_element_type=F32)


def _dot_tn(a, b):
    return lax.dot_general(a, b, TN_DIMS, preferred_element_type=F32)


def _split_dot(x, w):
    hi = x.astype(BF16)
    lo = (x - hi.astype(F32)).astype(BF16)
    return _dot(hi, w) + _dot(lo, w)


def _silu(x):
    return x * jax.nn.sigmoid(x)


def _layernorm(x, g, b):
    mu = jnp.mean(x, axis=-1, keepdims=True)
    xc = x - mu
    var = jnp.mean(xc * xc, axis=-1, keepdims=True)
    return xc * lax.rsqrt(var + EPS) * g + b


def _masked_softmax(s, mask):
    s = jnp.where(mask, s, NEG)
    m = jnp.max(s, axis=-1, keepdims=True)
    p = jnp.where(mask, jnp.exp(s - m), 0.0)
    return p / jnp.maximum(jnp.sum(p, axis=-1, keepdims=True), 1e-30)


def _norm_kernel(*refs, has_delta, write_x):
    it = iter(refs)
    x_ref = next(it)
    d_ref = next(it) if has_delta else None
    g_ref = next(it)
    xo_ref = next(it) if write_x else None
    h_ref = next(it)
    x = x_ref[...]
    if has_delta:
        x = x + d_ref[...]
    if write_x:
        xo_ref[...] = x
    y = x * lax.rsqrt(jnp.mean(x * x, axis=-1, keepdims=True) + EPS) * g_ref[...]
    h_ref[...] = y.astype(h_ref.dtype)


def add_norm(x, delta, g, *, write_x=True, out_dtype=BF16):
    m, d = x.shape
    row = pl.BlockSpec((TM_NORM, d), lambda i: (i, 0))
    ins = [x] + ([delta] if delta is not None else []) + [g.reshape(1, d)]
    in_specs = [row] * (len(ins) - 1) + [pl.BlockSpec((1, d), lambda i: (0, 0))]
    out_shape = ([jax.ShapeDtypeStruct((m, d), F32)] if write_x else []) + [jax.ShapeDtypeStruct((m, d), out_dtype)]
    outs = pl.pallas_call(
        functools.partial(_norm_kernel, has_delta=delta is not None, write_x=write_x),
        grid=(m // TM_NORM,), in_specs=in_specs, out_specs=[row] * len(out_shape), out_shape=out_shape,
        compiler_params=_cp(1), name="add_norm")(*ins)
    return (outs[0], outs[1]) if write_x else (None, outs[0])


def final_norm_rows(x, g, *, first_row, n_rows, tile):
    d = x.shape[1]
    first = first_row // tile
    return pl.pallas_call(
        functools.partial(_norm_kernel, has_delta=False, write_x=False), grid=(n_rows // tile,),
        in_specs=[pl.BlockSpec((tile, d), lambda i: (first + i, 0)), pl.BlockSpec((1, d), lambda i: (0, 0))],
        out_specs=pl.BlockSpec((tile, d), lambda i: (i, 0)), out_shape=jax.ShapeDtypeStruct((n_rows, d), F32),
        compiler_params=_cp(1), name="final_norm")(x, g.reshape(1, d))


def _ffn_kernel(h_ref, wg_ref, wu_ref, wo_ref, o_ref):
    @pl.when(pl.program_id(1) == 0)
    def _():
        o_ref[...] = jnp.zeros(o_ref.shape, F32)

    h = h_ref[...]
    g = _dot(h, wg_ref[...].astype(BF16))
    u = _dot(h, wu_ref[...].astype(BF16))
    a = (0.5 * _silu(g) * u).astype(BF16)
    for c0 in range(0, o_ref.shape[1], FFN_OUT_SLAB):
        cols = slice(c0, c0 + FFN_OUT_SLAB)
        o_ref[:, cols] += _dot(a, wo_ref[:, cols].astype(BF16))


def ffn_half(h, w_in, w_out, layer):
    m, d = h.shape
    nf = D_FF // FFN_CHUNK
    once = pl.Buffered(1)
    return pl.pallas_call(
        _ffn_kernel, grid=(m // TM, nf),
        in_specs=[pl.BlockSpec((TM, d), lambda i, f: (i, 0), pipeline_mode=once),
                  pl.BlockSpec((None, d, FFN_CHUNK), lambda i, f: (layer, 0, f)),
                  pl.BlockSpec((None, d, FFN_CHUNK), lambda i, f: (layer, 0, f + nf)),
                  pl.BlockSpec((None, FFN_CHUNK, d), lambda i, f: (layer, f, 0))],
        out_specs=pl.BlockSpec((TM, d), lambda i, f: (i, 0), pipeline_mode=once),
        out_shape=jax.ShapeDtypeStruct((m, d), F32), compiler_params=_cp(2), name="ffn")(h, w_in, w_in, w_out)


def _resident_bf16(w_ref, wb_ref):
    @pl.when(pl.program_id(1) == 0)
    def _():
        wb_ref[...] = w_ref[...].astype(BF16)

    return wb_ref[...]


def _mm_kernel(x_ref, w_ref, *refs, act, has_residual):
    res_ref = refs[0] if has_residual else None
    o_ref, scratch = refs[int(has_residual)], refs[int(has_residual) + 1:]
    w = _resident_bf16(w_ref, scratch[0]) if scratch else w_ref[...]
    acc = _dot(x_ref[...], w)
    if act == "sigmoid":
        acc = jax.nn.sigmoid(acc)
    if has_residual:
        acc = res_ref[...] + acc
    o_ref[...] = acc.astype(o_ref.dtype)


def _layer_block(shape, layer, index, *, single_buffer=False):
    mode = dict(pipeline_mode=pl.Buffered(1)) if single_buffer else {}
    return pl.BlockSpec((None,) + shape, lambda j, i: (layer,) + index(j, i), **mode)


def matmul(x, w, *, tn, layer=None, act=None, out_dtype=F32, residual=None):
    m, k = x.shape
    n = w.shape[-1]
    tile = pl.BlockSpec((TM, tn), lambda j, i: (i, j))
    extra, extra_specs = ([residual], [tile]) if residual is not None else ([], [])
    if layer is None:
        w_spec, scratch = pl.BlockSpec((k, tn), lambda j, i: (0, j)), []
    else:
        w_spec = _layer_block((k, tn), layer, lambda j, i: (0, j), single_buffer=True)
        scratch = [pltpu.VMEM((k, tn), BF16)]
    return pl.pallas_call(
        functools.partial(_mm_kernel, act=act, has_residual=residual is not None), grid=(n // tn, m // TM),
        in_specs=[pl.BlockSpec((TM, k), lambda j, i: (i, 0)), w_spec] + extra_specs,
        out_specs=tile, scratch_shapes=scratch,
        out_shape=jax.ShapeDtypeStruct((m, n), out_dtype), compiler_params=_cp(2), name="matmul")(x, w, *extra)


COL_TILE = 512


SUBLANES = 8


def _mm_window_kernel(x_ref, wt_ref, o_ref, wb_ref, *, act):
    @pl.when(pl.program_id(1) == 0)
    def _():
        wb_ref[...] = wt_ref[...].astype(BF16)

    acc = _dot_nt(x_ref[...], wb_ref[...])
    if act == "sigmoid":
        acc = jax.nn.sigmoid(acc)
    o_ref[...] = acc.astype(o_ref.dtype)


def matmul_window(x, w_t, layer, first_col, n_cols, *, act=None, out_dtype=F32):
    m, k = x.shape
    n_all = w_t.shape[1]
    first_row = layer * n_all + first_col
    assert first_row % SUBLANES == 0 and n_cols % COL_TILE == 0
    return pl.pallas_call(
        functools.partial(_mm_window_kernel, act=act), grid=(n_cols // COL_TILE, m // TM),
        in_specs=[pl.BlockSpec((TM, k), lambda j, i: (i, 0)),
                  pl.BlockSpec((pl.Element(COL_TILE), pl.Element(k)),
                               lambda j, i: (pl.multiple_of(first_row + j * COL_TILE, SUBLANES), 0))],
        out_specs=pl.BlockSpec((TM, COL_TILE), lambda j, i: (i, j)), scratch_shapes=[pltpu.VMEM((COL_TILE, k), BF16)],
        out_shape=jax.ShapeDtypeStruct((m, n_cols), out_dtype), compiler_params=_cp(2), name="matmul_window")(
            x, w_t.reshape(-1, k))


def _ple_kernel(h_ref, p_ref, wg_ref, wp_ref, res_ref, o_ref, wgb_ref):
    gate = jax.nn.sigmoid(_dot(h_ref[...], _resident_bf16(wg_ref, wgb_ref)))
    o_ref[...] = res_ref[...] + gate * _dot(p_ref[...], wp_ref[...].astype(BF16))


def ple_delta(h, ple, w_gate, w_pe, layer, residual, *, tn=512):
    m, k = h.shape
    n = w_gate.shape[-1]
    return pl.pallas_call(
        _ple_kernel, grid=(n // tn, m // TM),
        in_specs=[pl.BlockSpec((TM, k), lambda j, i: (i, 0)), pl.BlockSpec((TM, PLE_DIM), lambda j, i: (i, 0)),
                  _layer_block((k, tn), layer, lambda j, i: (0, j)),
                  _layer_block((PLE_DIM, tn), layer, lambda j, i: (0, j)),
                  pl.BlockSpec((TM, tn), lambda j, i: (i, j))],
        out_specs=pl.BlockSpec((TM, tn), lambda j, i: (i, j)), scratch_shapes=[pltpu.VMEM((k, tn), BF16)],
        out_shape=jax.ShapeDtypeStruct((m, n), F32), compiler_params=_cp(2), name="ple")(h, ple, w_gate, w_pe, residual)


def _merge_kernel(*refs):
    o_refs, w_refs, g_refs, out_ref, wb_refs = refs[0:4], refs[4:8], refs[8:12], refs[12], refs[13:17]
    acc = None
    for o_ref, w_ref, g_ref, wb_ref in zip(o_refs, w_refs, g_refs, wb_refs):
        term = g_ref[...].astype(F32) * _dot(o_ref[...], _resident_bf16(w_ref, wb_ref))
        acc = term if acc is None else acc + term
    out_ref[...] = acc.astype(out_ref.dtype)


def merge(branches, weights, gates, layer, *, tn=512):
    m = branches[0].shape[0]
    nb = D_MODEL // tn
    in_specs = ([pl.BlockSpec((TM, W_MIX), lambda j, i: (i, 0))] * 4
                + [_layer_block((W_MIX, tn), layer, lambda j, i: (0, j))] * 4
                + [pl.BlockSpec((TM, tn), lambda j, i, b=b: (i, b * nb + j)) for b in range(4)])
    return pl.pallas_call(
        _merge_kernel, grid=(nb, m // TM), in_specs=in_specs,
        out_specs=pl.BlockSpec((TM, tn), lambda j, i: (i, j)), scratch_shapes=[pltpu.VMEM((W_MIX, tn), BF16)] * 4,
        out_shape=jax.ShapeDtypeStruct((m, D_MODEL), BF16), compiler_params=_cp(2), name="merge")(
            *branches, *weights, gates, gates, gates, gates)


CONV_HALO = 32


def _conv_kernel(val_ref, gate_ref, halo_ref, w_ref, cb_ref, lg_ref, lb_ref, o_ref, st_ref, aext, *, tt, rb, n_last):
    t = pl.program_id(1)

    @pl.when(t == 0)
    def _():
        aext[0:CONV_HALO, :] = halo_ref[...]

    aext[CONV_HALO:CONV_HALO + tt, :] = val_ref[...] * jax.nn.sigmoid(gate_ref[...])
    first = CONV_HALO - (CONV_WIDTH - 1)
    for r0 in range(0, tt, rb):
        acc = jnp.zeros((rb, W_MIX), F32) + cb_ref[...]
        for j in range(CONV_WIDTH):
            acc = acc + aext[r0 + first + j:r0 + first + j + rb, :] * w_ref[j:j + 1, :]
        y = _silu(_layernorm(acc, lg_ref[...], lb_ref[...]))
        o_ref[r0:r0 + rb, :] = y.astype(o_ref.dtype)

    @pl.when(t == pl.num_programs(1) - 1)
    def _():
        st_ref[...] = aext[n_last:n_last + CONV_HALO, :]

    if tt >= CONV_HALO:
        aext[0:CONV_HALO, :] = aext[tt:tt + CONV_HALO, :]


def conv_mixer(z, halo, conv_w, conv_b, ln_g, ln_b, *, nb, t_len, tt, rb, n_last):
    nt = t_len // tt
    w = jnp.zeros((CONV_HALO, W_MIX), F32).at[:CONV_WIDTH].set(conv_w)
    vec = pl.BlockSpec((1, W_MIX), lambda b, t: (0, 0))
    return pl.pallas_call(
        functools.partial(_conv_kernel, tt=tt, rb=rb, n_last=n_last), grid=(nb, nt),
        in_specs=[pl.BlockSpec((tt, W_MIX), lambda b, t: (b * nt + t, C_VAL // W_MIX)),
                  pl.BlockSpec((tt, W_MIX), lambda b, t: (b * nt + t, C_GATE // W_MIX)),
                  pl.BlockSpec((None, CONV_HALO, W_MIX), lambda b, t: (b, 0, 0)),
                  pl.BlockSpec((CONV_HALO, W_MIX), lambda b, t: (0, 0)), vec, vec, vec],
        out_specs=[pl.BlockSpec((tt, W_MIX), lambda b, t: (b * nt + t, 0)),
                   pl.BlockSpec((None, CONV_HALO, W_MIX), lambda b, t: (b, 0, 0))],
        out_shape=[jax.ShapeDtypeStruct((nb * t_len, W_MIX), BF16), jax.ShapeDtypeStruct((nb, CONV_HALO, W_MIX), F32)],
        scratch_shapes=[pltpu.VMEM((CONV_HALO + tt, W_MIX), F32)],
        compiler_params=_cp(2), name="conv")(z, z, halo, w, conv_b.reshape(1, -1), ln_g.reshape(1, -1), ln_b.reshape(1, -1))


def _sgu_kernel(u_ref, v_ref, wc_ref, bs_ref, lg_ref, lb_ref, o_ref, *, tt):
    for c0 in range(0, tt, SGU_CHUNK):
        rows = slice(c0, c0 + SGU_CHUNK)
        u = jax.nn.gelu(u_ref[rows, :])
        v = _layernorm(jax.nn.gelu(v_ref[rows, :]), lg_ref[...], lb_ref[...]).astype(BF16)
        parts = [_dot(wc_ref[g], v[:, g * LANES:(g + 1) * LANES]) for g in range(SGU_GROUPS)]
        mix = jnp.concatenate(parts, axis=1) + bs_ref[...]
        o_ref[rows, :] = (u * mix).astype(o_ref.dtype)


def sgu_mixer(z, w_causal, bias_rows, ln_g, ln_b, *, rows, tt=256):
    vec = pl.BlockSpec((1, W_MIX), lambda i: (0, 0))
    return pl.pallas_call(
        functools.partial(_sgu_kernel, tt=tt), grid=(rows // tt,),
        in_specs=[pl.BlockSpec((tt, W_MIX), lambda i: (i, C_U // W_MIX)),
                  pl.BlockSpec((tt, W_MIX), lambda i: (i, C_V // W_MIX)),
                  pl.BlockSpec((SGU_GROUPS, SGU_CHUNK, SGU_CHUNK), lambda i: (0, 0, 0)),
                  pl.BlockSpec((SGU_CHUNK, W_MIX), lambda i: (0, 0)), vec, vec],
        out_specs=pl.BlockSpec((tt, W_MIX), lambda i: (i, 0)),
        out_shape=jax.ShapeDtypeStruct((rows, W_MIX), BF16), compiler_params=_cp(1), name="sgu")(
            z, z, w_causal, bias_rows, ln_g.reshape(1, -1), ln_b.reshape(1, -1))


def _sgu_step_kernel(u_ref, v_ref, w0_ref, b0_ref, lg_ref, lb_ref, o_ref, vo_ref):
    u = jax.nn.gelu(u_ref[...])
    v = _layernorm(jax.nn.gelu(v_ref[...]), lg_ref[...], lb_ref[...])
    vo_ref[...] = v
    o_ref[...] = (u * (v * w0_ref[...] + b0_ref[...])).astype(o_ref.dtype)


def sgu_step(zs, w0, b0, ln_g, ln_b):
    n = zs.shape[0]
    vec = pl.BlockSpec((1, W_MIX), lambda i: (0, 0))
    return pl.pallas_call(
        _sgu_step_kernel, grid=(1,),
        in_specs=[pl.BlockSpec((n, W_MIX), lambda i: (0, C_U // W_MIX)),
                  pl.BlockSpec((n, W_MIX), lambda i: (0, C_V // W_MIX)), vec, vec, vec, vec],
        out_specs=[pl.BlockSpec((n, W_MIX), lambda i: (0, 0))] * 2,
        out_shape=[jax.ShapeDtypeStruct((n, W_MIX), BF16), jax.ShapeDtypeStruct((n, W_MIX), F32)],
        compiler_params=_cp(1), name="sgu_step")(zs, zs, w0, b0, ln_g.reshape(1, -1), ln_b.reshape(1, -1))


def _log_sigmoid(x):
    return jnp.minimum(x, 0.0) - jnp.log1p(jnp.exp(-jnp.abs(x)))


GLA_PAIRS = GLA_HEADS // 2
GLA_KEYS = GLA_HEADS * GLA_DK


def _gla_kernel(q_ref, k_ref, v_ref, r_ref, da_ref, wa_ref, ba_ref, gn_ref, s0_ref, o_ref, so_ref, st, *, tt, chunk, n_valid):
    t = pl.program_id(1)

    @pl.when(t == 0)
    def _():
        st[...] = s0_ref[...]

    row = lax.broadcasted_iota(jnp.int32, (chunk, chunk), 0)
    col = lax.broadcasted_iota(jnp.int32, (chunk, chunk), 1)
    causal = row >= col
    tril = causal.astype(BF16)
    low_half = lax.broadcasted_iota(jnp.int32, (chunk, LANES), 1) < GLA_DK
    for c0 in range(0, tt, chunk):
        rows = slice(c0, c0 + chunk)
        la = _log_sigmoid(_dot(da_ref[rows, :].astype(BF16), wa_ref[...]) + ba_ref[...]) * (1.0 / GLA_TAU)
        if n_valid < chunk:
            la = jnp.where(lax.broadcasted_iota(jnp.int32, la.shape, 0) < n_valid, la, 0.0)
        b = _split_dot_left(tril, la)
        bl = b[chunk - 1:chunk, :]
        kk = k_ref[rows, :]
        q_in = q_ref[rows, :] * (GLA_DK ** -0.5) * jnp.exp(b)
        k_in = (kk * jnp.exp(-b)).astype(BF16)
        k_end = kk * jnp.exp(bl - b)
        decay = jnp.exp(bl)
        vv = v_ref[rows, :].astype(BF16)
        for p in range(GLA_PAIRS):
            ps = slice(p * LANES, (p + 1) * LANES)
            state = st[p]
            state_b = state.astype(BF16)
            update = state * decay[:, ps]
            for e in range(2):
                hs = slice((2 * p + e) * GLA_DV, (2 * p + e + 1) * GLA_DV)
                mine = low_half if e == 0 else jnp.logical_not(low_half)
                q_h = jnp.where(mine, q_in[:, ps], 0.0).astype(BF16)
                k_h = jnp.where(mine, k_end[:, ps], 0.0).astype(BF16)
                a = jnp.where(causal, _dot_nt(q_h, k_in[:, ps]), 0.0).astype(BF16)
                o = _dot(a, vv[:, hs]) + _dot_nt(q_h, state_b)
                update = update + _dot_tn(vv[:, hs], k_h)
                y = o * lax.rsqrt(jnp.mean(o * o, axis=-1, keepdims=True) + EPS) * gn_ref[...]
                o_ref[rows, hs] = (y * _silu(r_ref[rows, hs])).astype(o_ref.dtype)
            st[p] = update

    @pl.when(t == pl.num_programs(1) - 1)
    def _():
        so_ref[...] = st[...]


def _split_dot_left(w, x):
    hi = x.astype(BF16)
    lo = (x - hi.astype(F32)).astype(BF16)
    return _dot(w, hi) + _dot(w, lo)


def gla_state_in(s):
    nb = s.shape[0]
    s = s.reshape(nb, GLA_PAIRS, 2, GLA_DK, GLA_DV)
    return jnp.transpose(s, (0, 1, 4, 2, 3)).reshape(nb, GLA_PAIRS, GLA_DV, 2 * GLA_DK)


def gla_state_out(s):
    nb = s.shape[0]
    s = s.reshape(nb, GLA_PAIRS, GLA_DV, 2, GLA_DK)
    return jnp.transpose(s, (0, 1, 3, 4, 2)).reshape(nb, GLA_HEADS, GLA_DK, GLA_DV)


def gla_mixer(z2, z3, s0_t, w_a2p, b_a, g_norm, *, nb, t_len, tt, chunk, n_valid):
    nt = t_len // tt
    rows = lambda width, c: pl.BlockSpec((tt, width), lambda b, t: (b * nt + t, c // width))
    state = pl.BlockSpec((None, GLA_PAIRS, GLA_DV, LANES), lambda b, t: (b, 0, 0, 0))
    return pl.pallas_call(
        functools.partial(_gla_kernel, tt=tt, chunk=chunk, n_valid=n_valid), grid=(nb, nt),
        in_specs=[rows(GLA_KEYS, C2_DQ), rows(GLA_KEYS, C2_DK), rows(W_MIX, C2_DV), rows(W_MIX, 0), rows(LANES, C2_DA),
                  pl.BlockSpec((LANES, GLA_KEYS), lambda b, t: (0, 0)),
                  pl.BlockSpec((1, GLA_KEYS), lambda b, t: (0, 0)),
                  pl.BlockSpec((1, LANES), lambda b, t: (0, 0)), state],
        out_specs=[pl.BlockSpec((tt, W_MIX), lambda b, t: (b * nt + t, 0)), state],
        out_shape=[jax.ShapeDtypeStruct((nb * t_len, W_MIX), BF16),
                   jax.ShapeDtypeStruct((nb, GLA_PAIRS, GLA_DV, LANES), F32)],
        scratch_shapes=[pltpu.VMEM((GLA_PAIRS, GLA_DV, LANES), F32)],
        compiler_params=_cp(2), name="gla")(z2, z2, z2, z3, z2, w_a2p, b_a.reshape(1, -1), g_norm.reshape(1, -1), s0_t)


SEG_PER_STEP = 128
CMP_ROWS = SEG_PER_STEP * CMP_STRIDE


def _compress_kernel(*refs, paged):
    if paged:
        n_in = PAGES_PER_STEP + 1
        in_refs = refs[1:1 + n_in]
        w_ref, o_ref = refs[1 + n_in], refs[2 + n_in]
        xs = refs[3 + n_in:]
    else:
        x_ref, w_ref, o_ref = refs[0:3]
        xs = refs[3:]
    for ch in range(4):
        plane, head = divmod(ch, KVH)
        cols = slice(ch * LANES, (ch + 1) * LANES)
        if paged:
            for j in range(PAGES_PER_STEP):
                xs[ch][j * PAGE_SIZE:(j + 1) * PAGE_SIZE, :] = in_refs[j][pl.ds(ch, PAGE_SIZE, stride=4), :]
            xs[ch][CMP_ROWS:CMP_ROWS + CMP_STRIDE, :] = in_refs[PAGES_PER_STEP][pl.ds(ch, CMP_STRIDE, stride=4), :]
        else:
            xs[ch][0:CMP_ROWS, :] = x_ref[:, cols]
            xs[ch][CMP_ROWS:CMP_ROWS + CMP_STRIDE, :] = jnp.zeros((CMP_STRIDE, LANES), F32)
        acc = jnp.zeros((SEG_PER_STEP, LANES), F32)
        for r in range(CMP_LEN):
            x = xs[ch][pl.ds(r, SEG_PER_STEP, stride=CMP_STRIDE), :]
            acc = acc + _dot(x.astype(BF16), w_ref[plane, r])
        o_ref[:, cols] = acc


def compress_prompt(z, w_cmp):
    return pl.pallas_call(
        functools.partial(_compress_kernel, paged=False), grid=(BATCH,),
        in_specs=[pl.BlockSpec((CMP_ROWS, 4 * LANES), lambda b: (b, C_CMP // (4 * LANES))),
                  pl.BlockSpec((2, CMP_LEN, LANES, LANES), lambda b: (0, 0, 0, 0))],
        out_specs=pl.BlockSpec((None, SEG_PER_STEP, 4 * LANES), lambda b: (b, 0, 0)),
        out_shape=jax.ShapeDtypeStruct((BATCH, SEG_PER_STEP, 4 * LANES), F32),
        scratch_shapes=[pltpu.VMEM((CMP_ROWS + CMP_STRIDE, LANES), F32)] * 4,
        compiler_params=_cp(1), name="compress_prompt")(z, w_cmp)


PAGES_PER_STEP = CMP_ROWS // PAGE_SIZE
N_PAGES = PAST_LEN // PAGE_SIZE


def _cache_rows(cache):
    return cache.reshape(cache.shape[0], cache.shape[1], 4 * PAGE_SIZE, HEAD_DIM)


def _page_spec(layer, page_of):
    return pl.BlockSpec((None, None, 4 * PAGE_SIZE, HEAD_DIM), lambda b, c, pt: (layer, page_of(b, c, pt), 0, 0))


def compress_paged(cache, page_table, w_cmp, layer):
    n_steps = N_PAGES // PAGES_PER_STEP
    specs = [_page_spec(layer, lambda b, c, pt, j=j: pt[b, jnp.minimum(c * PAGES_PER_STEP + j, N_PAGES - 1)])
             for j in range(PAGES_PER_STEP + 1)]
    grid_spec = pltpu.PrefetchScalarGridSpec(
        num_scalar_prefetch=1, grid=(DEC_BATCH, n_steps),
        in_specs=specs + [pl.BlockSpec((2, CMP_LEN, LANES, LANES), lambda b, c, pt: (0, 0, 0, 0))],
        out_specs=pl.BlockSpec((None, SEG_PER_STEP, 4 * LANES), lambda b, c, pt: (b, c, 0)),
        scratch_shapes=[pltpu.VMEM((CMP_ROWS + CMP_STRIDE, LANES), F32)] * 4)
    return pl.pallas_call(
        functools.partial(_compress_kernel, paged=True), grid_spec=grid_spec,
        out_shape=jax.ShapeDtypeStruct((DEC_BATCH, n_steps * SEG_PER_STEP, 4 * LANES), F32),
        compiler_params=_cp(2), name="compress_paged")(page_table, *([cache] * len(specs)), w_cmp)


def _top_select(score, n_lanes):
    lane = lax.broadcasted_iota(jnp.int32, score.shape, 1).astype(F32)
    sel = jnp.zeros(score.shape, F32)
    for _ in range(N_SELECT):
        m = jnp.max(score, axis=-1, keepdims=True)
        idx = jnp.min(jnp.where(score == m, lane, float(n_lanes)), axis=-1, keepdims=True)
        pick = lane == idx
        sel = jnp.where(pick, 1.0, sel)
        score = jnp.where(pick, -3e38, score)
    return sel


def _bias_tiles(delta_of_tile, n_tiles, t0, t1, t2):
    tiles = []
    for j in range(n_tiles):
        delta = delta_of_tile(j)
        tiles.append(jnp.where(delta == 0, t0, jnp.where(delta == 1, t1, t2)))
    return jnp.concatenate(tiles, axis=1)


WIN_BAND = WINDOW + Q_BLOCK


SLC_KEY_STEP = 512


def _nsa_prompt_kernel(q_ref, cbk_ref, cbv_ref, ks_ref, vs_ref, kw_ref, vw_ref, g_ref, bc_ref, t0_ref, t1_ref, t2_ref,
                       covt_ref, e_ref, o_ref, oslc_ref):
    h = pl.program_id(1)
    i = pl.program_id(2)
    q = q_ref[...] * (HEAD_DIM ** -0.5)
    qs = jnp.concatenate([q[:, g * LANES:(g + 1) * LANES] for g in range(GRP)], axis=0).astype(BF16)
    qpos = i * Q_BLOCK + lax.broadcasted_iota(jnp.int32, (Q_BLOCK, 1), 0)
    lane = lax.broadcasted_iota(jnp.int32, (Q_BLOCK, LANES), 1)

    sc = _dot_nt(qs, cbk_ref[...].astype(BF16))
    cmask = (lane * CMP_STRIDE + (CMP_LEN - 1) <= qpos) & (lane < SEQ // CMP_STRIDE - 1)
    p_c = [_masked_softmax(sc[g * Q_BLOCK:(g + 1) * Q_BLOCK] + bc_ref[g], cmask) for g in range(GRP)]
    o_cmp = _dot(jnp.concatenate(p_c, axis=0).astype(BF16), cbv_ref[...].astype(BF16))
    psum = p_c[0] + p_c[1] + p_c[2] + p_c[3]
    p_hi = psum.astype(BF16)
    p_lo = (psum - p_hi.astype(F32)).astype(BF16)
    imp_t = _dot_nt(covt_ref[...], p_hi) + _dot_nt(covt_ref[...], p_lo)

    n_blk = SEQ // SLC_LEN
    blk = lax.broadcasted_iota(jnp.int32, (n_blk, Q_BLOCK), 0)
    qpos_l = i * Q_BLOCK + lax.broadcasted_iota(jnp.int32, (n_blk, Q_BLOCK), 1)
    cur = lax.shift_right_logical(qpos_l, 6)
    valid = blk * SLC_LEN <= qpos_l
    forced = (blk == 0) | (blk == cur) | (blk == cur - 1)
    score = jnp.where(valid, imp_t[0:n_blk] + jnp.where(forced, FORCE_BONUS, 0.0), NEG)
    rank = jnp.zeros((n_blk, Q_BLOCK), F32)
    for other_blk in range(n_blk):
        other = score[other_blk:other_blk + 1, :]
        rank = rank + jnp.where((other > score) | ((other == score) & (blk > other_blk)), 1.0, 0.0)
    sel_t = jnp.where(valid & (rank < N_SELECT), 1.0, 0.0).astype(BF16)

    def selected(n_keys):
        kpos = lax.broadcasted_iota(jnp.int32, (Q_BLOCK, n_keys), 1)
        smask = (_dot_tn(sel_t, e_ref[0:n_blk, 0:n_keys]) > 0.5) & (kpos <= qpos)
        ss = _dot_nt(qs, ks_ref[0:n_keys, :].astype(BF16))
        p_s = []
        for g in range(GRP):
            bias = _bias_tiles(lambda j: i - j, n_keys // Q_BLOCK, t0_ref[g], t1_ref[g], t2_ref[g])
            p_s.append(_masked_softmax(ss[g * Q_BLOCK:(g + 1) * Q_BLOCK] + bias, smask))
        oslc_ref[...] = _dot(jnp.concatenate(p_s, axis=0).astype(BF16), vs_ref[0:n_keys, :].astype(BF16))

    tiles_per_step = SLC_KEY_STEP // Q_BLOCK
    for hi in range(tiles_per_step, SEQ // Q_BLOCK + 1, tiles_per_step):
        pl.when((i >= hi - tiles_per_step) & (i < hi))(functools.partial(selected, hi * Q_BLOCK))
    o_slc = oslc_ref[...]

    first_tile = jnp.maximum(i - WINDOW // Q_BLOCK, 0)
    start = pl.multiple_of(first_tile * Q_BLOCK, Q_BLOCK)
    dist = qpos - (start + lax.broadcasted_iota(jnp.int32, (Q_BLOCK, WIN_BAND), 1))
    wmask = (dist >= 0) & (dist <= WINDOW)
    sw = _dot_nt(qs, kw_ref[pl.ds(start, WIN_BAND), :].astype(BF16))
    p_w = []
    for g in range(GRP):
        bias = _bias_tiles(lambda j: i - first_tile - j, WIN_BAND // Q_BLOCK, t0_ref[g], t1_ref[g], t2_ref[g])
        p_w.append(_masked_softmax(sw[g * Q_BLOCK:(g + 1) * Q_BLOCK] + bias, wmask))
    o_win = _dot(jnp.concatenate(p_w, axis=0).astype(BF16), vw_ref[pl.ds(start, WIN_BAND), :].astype(BF16))

    gates = jax.nn.sigmoid(g_ref[...])
    for g in range(GRP):
        rows = slice(g * Q_BLOCK, (g + 1) * Q_BLOCK)
        base = (h * GRP + g) * 3
        gc = [jnp.sum(jnp.where(lane == base + br, gates, 0.0), axis=-1, keepdims=True) for br in range(3)]
        o = gc[0] * o_cmp[rows] + gc[1] * o_slc[rows] + gc[2] * o_win[rows]
        o_ref[:, g * LANES:(g + 1) * LANES] = o.astype(o_ref.dtype)


def nsa_prompt(z, cb, bias_c, t0, t1, t2, cover_t, expand):
    nq = SEQ // Q_BLOCK
    kv = lambda c: pl.BlockSpec((SEQ, LANES), lambda b, h, i, c=c: (b, c // LANES + h))
    cbs = lambda c: pl.BlockSpec((None, SEG_PER_STEP, LANES), lambda b, h, i, c=c: (b, 0, c + h))
    toe = pl.BlockSpec((GRP, Q_BLOCK, LANES), lambda b, h, i: (h, 0, 0))
    return pl.pallas_call(
        _nsa_prompt_kernel, grid=(BATCH, KVH, nq),
        in_specs=[pl.BlockSpec((Q_BLOCK, GRP * LANES), lambda b, h, i: (b * nq + i, C_Q // (GRP * LANES) + h)),
                  cbs(0), cbs(KVH), kv(C_SLC), kv(C_SLC + KVH * LANES), kv(C_WIN), kv(C_WIN + KVH * LANES),
                  pl.BlockSpec((Q_BLOCK, LANES), lambda b, h, i: (b * nq + i, C_CG // LANES)),
                  pl.BlockSpec((GRP, Q_BLOCK, LANES), lambda b, h, i: (h, i, 0)), toe, toe, toe,
                  pl.BlockSpec((LANES, LANES), lambda b, h, i: (0, 0)),
                  pl.BlockSpec((LANES, SEQ), lambda b, h, i: (0, 0))],
        out_specs=pl.BlockSpec((Q_BLOCK, GRP * LANES), lambda b, h, i: (b * nq + i, h)),
        scratch_shapes=[pltpu.VMEM((GRP * Q_BLOCK, LANES), F32)],
        out_shape=jax.ShapeDtypeStruct((M_PROMPT, W_MIX), BF16), compiler_params=_cp(3), name="nsa_prompt")(
            z, cb, cb, z, z, z, z, z, bias_c, t0, t1, t2, cover_t, expand)


N_CMP_S = PAST_LEN // CMP_STRIDE
N_SLC_S = 384


def _head_rows(shape):
    return lax.broadcasted_iota(jnp.int32, shape, 0) < GRP


def _nsa_step_a_kernel(q_ref, cb_ref, bc_ref, cov_ref, wb_ref, kvn_ref, bw_ref, bwn_ref, ocw_ref, sel_ref):
    q = q_ref[...] * (HEAD_DIM ** -0.5)
    qb = q.astype(BF16)
    cb = cb_ref[...]
    first = _head_rows((NSA_HEADS, N_CMP_S))
    s = jnp.where(first, _dot_nt(qb, cb[:, 0:LANES].astype(BF16)), _dot_nt(qb, cb[:, LANES:2 * LANES].astype(BF16)))
    kidx = lax.broadcasted_iota(jnp.int32, (NSA_HEADS, N_CMP_S), 1)
    p = _masked_softmax(s + bc_ref[...], kidx * CMP_STRIDE + (CMP_LEN - 1) <= PAST_LEN)
    pb = p.astype(BF16)
    o_cmp = jnp.where(_head_rows((NSA_HEADS, LANES)), _dot(pb, cb[:, 2 * LANES:3 * LANES].astype(BF16)),
                      _dot(pb, cb[:, 3 * LANES:4 * LANES].astype(BF16)))
    row = lax.broadcasted_iota(jnp.int32, (NSA_HEADS, N_CMP_S), 0)
    ps0 = jnp.sum(jnp.where(first, p, 0.0), axis=0, keepdims=True)
    ps1 = jnp.sum(jnp.where(first, 0.0, p), axis=0, keepdims=True)
    psum = jnp.where(row == 0, ps0, jnp.where(row == 1, ps1, 0.0))
    imp = _split_dot(psum, cov_ref[...])
    lane = lax.broadcasted_iota(jnp.int32, (NSA_HEADS, N_SLC_S), 1)
    cur = PAST_LEN // SLC_LEN
    valid = lane <= cur
    forced = (lane == 0) | (lane == cur) | (lane == cur - 1)
    score = jnp.where(valid, imp + jnp.where(forced, FORCE_BONUS, 0.0), NEG)
    sel_ref[...] = _top_select(score, N_SLC_S) * valid.astype(F32)

    wb = wb_ref[...]
    firstw = _head_rows((NSA_HEADS, WINDOW))
    sw = jnp.where(firstw, _dot_nt(qb, wb[:, 0:LANES].astype(BF16)), _dot_nt(qb, wb[:, LANES:2 * LANES].astype(BF16)))
    sw = sw + bw_ref[...]
    firstl = _head_rows((NSA_HEADS, LANES))
    kvn = kvn_ref[...]
    k_new = jnp.where(firstl, kvn[:, 0:LANES], kvn[:, LANES:2 * LANES])
    v_new = jnp.where(firstl, kvn[:, 2 * LANES:3 * LANES], kvn[:, 3 * LANES:4 * LANES])
    s_new = jnp.sum(q * k_new, axis=-1, keepdims=True) + bwn_ref[:, 0:1]
    m = jnp.maximum(jnp.max(sw, axis=-1, keepdims=True), s_new)
    pw = jnp.exp(sw - m)
    pn = jnp.exp(s_new - m)
    pwb = pw.astype(BF16)
    acc = jnp.where(firstl, _dot(pwb, wb[:, 2 * LANES:3 * LANES].astype(BF16)), _dot(pwb, wb[:, 3 * LANES:4 * LANES].astype(BF16)))
    o_win = (acc + pn * v_new) / (jnp.sum(pw, axis=-1, keepdims=True) + pn)
    ocw_ref[:, 0:LANES] = o_cmp
    ocw_ref[:, LANES:2 * LANES] = o_win


def nsa_step_a(q_s, cb_s, bias_c, cover, win_buf, kvw_new, bias_w, bias_wn):
    per_b = lambda *shape: pl.BlockSpec((None,) + shape, lambda b: (b,) + (0,) * len(shape))
    whole = lambda *shape: pl.BlockSpec(shape, lambda b: (0,) * len(shape))
    return pl.pallas_call(
        _nsa_step_a_kernel, grid=(DEC_BATCH,),
        in_specs=[per_b(NSA_HEADS, LANES), per_b(N_CMP_S, 4 * LANES), whole(NSA_HEADS, N_CMP_S), whole(N_CMP_S, N_SLC_S),
                  per_b(WINDOW, 4 * LANES), per_b(1, 4 * LANES), whole(NSA_HEADS, WINDOW), whole(NSA_HEADS, LANES)],
        out_specs=[per_b(NSA_HEADS, 2 * LANES), per_b(NSA_HEADS, N_SLC_S)],
        out_shape=[jax.ShapeDtypeStruct((DEC_BATCH, NSA_HEADS, 2 * LANES), F32),
                   jax.ShapeDtypeStruct((DEC_BATCH, NSA_HEADS, N_SLC_S), F32)],
        compiler_params=_cp(1), name="nsa_step_a")(q_s, cb_s, bias_c, cover, win_buf, kvw_new, bias_w, bias_wn)


def _nsa_step_b_kernel(*refs):
    pages = refs[1:1 + PAGES_PER_STEP]
    (q_ref, sel_ref, e_ref, bs_ref, kvn_ref, seln_ref, bn_ref, ocw_ref, g_ref, o_ref, m_sc, l_sc, acc_sc) = refs[1 + PAGES_PER_STEP:]
    c = pl.program_id(1)

    @pl.when(c == 0)
    def _():
        m_sc[...] = jnp.full(m_sc.shape, NEG, F32)
        l_sc[...] = jnp.zeros(l_sc.shape, F32)
        acc_sc[...] = jnp.zeros(acc_sc.shape, F32)

    q = q_ref[...] * (HEAD_DIM ** -0.5)
    qb = q.astype(BF16)
    kv = [jnp.concatenate([pg[pl.ds(ch, PAGE_SIZE, stride=4), :].astype(BF16) for pg in pages], axis=0)
          for ch in range(4)]
    first = _head_rows((NSA_HEADS, CMP_ROWS))
    firstl = _head_rows((NSA_HEADS, LANES))
    s = jnp.where(first, _dot_nt(qb, kv[0]), _dot_nt(qb, kv[1])) + bs_ref[...]
    mask = _dot(sel_ref[...].astype(BF16), e_ref[...]) > 0.5
    s = jnp.where(mask, s, NEG)
    m_old = m_sc[...]
    m_new = jnp.maximum(m_old, jnp.max(s, axis=-1, keepdims=True))
    alpha = jnp.exp(m_old - m_new)
    p = jnp.where(mask, jnp.exp(s - m_new), 0.0)
    pb = p.astype(BF16)
    l_sc[...] = alpha * l_sc[...] + jnp.sum(p, axis=-1, keepdims=True)
    acc_sc[...] = alpha * acc_sc[...] + jnp.where(firstl, _dot(pb, kv[2]), _dot(pb, kv[3]))
    m_sc[...] = m_new

    @pl.when(c == pl.num_programs(1) - 1)
    def _():
        kvn = kvn_ref[...]
        k_new = jnp.where(firstl, kvn[:, 0:LANES], kvn[:, LANES:2 * LANES])
        v_new = jnp.where(firstl, kvn[:, 2 * LANES:3 * LANES], kvn[:, 3 * LANES:4 * LANES])
        on = seln_ref[:, 0:1] > 0.5
        s_n = jnp.where(on, jnp.sum(q * k_new, axis=-1, keepdims=True) + bn_ref[:, 0:1], NEG)
        m_old = m_sc[...]
        m_fin = jnp.maximum(m_old, s_n)
        alpha = jnp.exp(m_old - m_fin)
        p_n = jnp.where(on, jnp.exp(s_n - m_fin), 0.0)
        l_fin = alpha * l_sc[...] + p_n
        o_slc = (alpha * acc_sc[...] + p_n * v_new) / jnp.maximum(l_fin, 1e-30)
        gates = jax.nn.sigmoid(g_ref[...])
        o = gates[:, 0:1] * ocw_ref[:, 0:LANES] + gates[:, 1:2] * o_slc + gates[:, 2:3] * ocw_ref[:, LANES:2 * LANES]
        o_ref[...] = o.astype(o_ref.dtype)


def nsa_step_b(cache, page_table, layer, q_s, sel_r, expand, bias_s, kvs_new, sel_new, bias_n, ocw, gates):
    n_steps = N_PAGES // PAGES_PER_STEP
    page_specs = [_page_spec(layer, lambda b, c, pt, j=j: pt[b, c * PAGES_PER_STEP + j]) for j in range(PAGES_PER_STEP)]
    per_b = lambda *shape: pl.BlockSpec((None,) + shape, lambda b, c, pt: (b,) + (0,) * len(shape))
    whole = lambda *shape: pl.BlockSpec(shape, lambda b, c, pt: (0,) * len(shape))
    grid_spec = pltpu.PrefetchScalarGridSpec(
        num_scalar_prefetch=1, grid=(DEC_BATCH, n_steps),
        in_specs=page_specs + [
            per_b(NSA_HEADS, LANES),
            pl.BlockSpec((None, None, NSA_HEADS, LANES), lambda b, c, pt: (b, c, 0, 0)),
            whole(LANES, CMP_ROWS),
            pl.BlockSpec((NSA_HEADS, CMP_ROWS), lambda b, c, pt: (0, c)),
            per_b(1, 4 * LANES), per_b(NSA_HEADS, LANES), whole(NSA_HEADS, LANES),
            per_b(NSA_HEADS, 2 * LANES), per_b(NSA_HEADS, LANES)],
        out_specs=per_b(NSA_HEADS, LANES),
        scratch_shapes=[pltpu.VMEM((NSA_HEADS, 1), F32), pltpu.VMEM((NSA_HEADS, 1), F32), pltpu.VMEM((NSA_HEADS, LANES), F32)])
    return pl.pallas_call(
        _nsa_step_b_kernel, grid_spec=grid_spec,
        out_shape=jax.ShapeDtypeStruct((DEC_BATCH, NSA_HEADS, LANES), BF16),
        compiler_params=_cp(2), name="nsa_step_b")(
            page_table, *([cache] * len(page_specs)), q_s, sel_r, expand, bias_s, kvs_new, sel_new, bias_n, ocw, gates)


def _rel_bias_of(rel_bias, dist):
    n = jnp.maximum(dist, 0)
    max_exact = REL_BUCKETS // 2
    nf = jnp.maximum(n, 1).astype(F32)
    large = max_exact + (jnp.log(nf / max_exact) / math.log(REL_MAX_DIST / max_exact)
                         * (REL_BUCKETS - max_exact)).astype(jnp.int32)
    bucket = jnp.where(n < max_exact, n, jnp.minimum(large, REL_BUCKETS - 1))[None]
    out = jnp.zeros((NSA_HEADS,) + dist.shape, F32)
    for k in range(REL_BUCKETS):
        out = jnp.where(bucket == k, rel_bias[k].reshape((NSA_HEADS,) + (1,) * dist.ndim), out)
    return out


def _nsa_tables(rel_bias):
    bias_of = functools.partial(_rel_bias_of, rel_bias)
    r = jnp.arange(Q_BLOCK)
    diff = r[:, None] - r[None, :]
    t0 = bias_of(diff)
    t1 = bias_of(Q_BLOCK + diff)
    t2 = bias_of(2 * Q_BLOCK + diff)
    bias_c = bias_of(jnp.arange(SEQ)[:, None] - (jnp.arange(LANES)[None, :] * CMP_STRIDE + CMP_LEN - 1))
    k = jnp.arange(LANES)
    j = jnp.arange(LANES)
    cover = ((k[:, None] * CMP_STRIDE < (j[None, :] + 1) * SLC_LEN) & (k[:, None] * CMP_STRIDE + CMP_LEN > j[None, :] * SLC_LEN)
             & (k[:, None] < SEQ // CMP_STRIDE - 1) & (j[None, :] < SEQ // SLC_LEN)).astype(BF16)
    expand = (jnp.arange(SEQ)[None, :] // SLC_LEN == jnp.arange(LANES)[:, None]).astype(BF16)
    ks = jnp.arange(N_CMP_S)
    bias_cs = bias_of(PAST_LEN - (ks * CMP_STRIDE + CMP_LEN - 1))
    js = jnp.arange(N_SLC_S)
    cover_s = ((ks[:, None] * CMP_STRIDE < (js[None, :] + 1) * SLC_LEN) & (ks[:, None] * CMP_STRIDE + CMP_LEN > js[None, :] * SLC_LEN)
               & (ks[:, None] < N_CMP_S - 1) & (js[None, :] <= PAST_LEN // SLC_LEN)).astype(BF16)
    bias_w = bias_of(WINDOW - jnp.arange(WINDOW))
    bias_0 = bias_of(jnp.zeros((LANES,), jnp.int32))
    bias_s = bias_of(PAST_LEN - jnp.arange(PAST_LEN))
    return dict(t0=t0, t1=t1, t2=t2, bias_c=bias_c, cover_t=cover.T, expand=expand, bias_cs=bias_cs, cover_s=cover_s,
                bias_w=bias_w, bias_0=bias_0, bias_s=bias_s)


def _kv_export_kernel(c_ref, s_ref, w_ref, co_ref, so_ref, wo_ref):
    co_ref[...] = c_ref[...]
    so_ref[...] = s_ref[...]
    wo_ref[...] = w_ref[...]


def kv_export(z1):
    kv_w = 4 * LANES
    full = lambda c: pl.BlockSpec((SEQ, kv_w), lambda b: (b, c // kv_w))
    out_full = pl.BlockSpec((None, SEQ, kv_w), lambda b: (b, 0, 0))
    return pl.pallas_call(
        _kv_export_kernel, grid=(BATCH,),
        in_specs=[full(C_CMP), full(C_SLC),
                  pl.BlockSpec((WINDOW, kv_w), lambda b: ((b + 1) * (SEQ // WINDOW) - 1, C_WIN // kv_w))],
        out_specs=[out_full, out_full, pl.BlockSpec((None, WINDOW, kv_w), lambda b: (b, 0, 0))],
        out_shape=[jax.ShapeDtypeStruct((BATCH, SEQ, kv_w), F32)] * 2 + [jax.ShapeDtypeStruct((BATCH, WINDOW, kv_w), F32)],
        compiler_params=_cp(1), name="kv_export")(z1, z1, z1)


def _mixers(layer, z1, z2, z3, tabs, wts, st):
    new = {}
    sample = slice(M_PROMPT, M_PROMPT + DEC_BATCH)
    zs = z1[sample]

    def one_tile_per_batch(a):
        return jnp.zeros((DEC_BATCH, 8, a.shape[1]), F32).at[:, 0].set(a).reshape(DEC_BATCH * 8, a.shape[1])

    tail = jnp.zeros((M_SLAB - M_PROMPT - DEC_BATCH, W_MIX), BF16)
    slab = lambda p, s: jnp.concatenate([p, s, tail], axis=0)

    zero_halo = jnp.zeros((BATCH, CONV_HALO, W_MIX), F32)
    conv_args = (wts['conv_w'], wts['conv_b'], wts['conv_ln_g'], wts['conv_ln_b'])
    oa_p, cst_p = conv_mixer(z1, zero_halo, *conv_args, nb=BATCH, t_len=SEQ, tt=256, rb=32, n_last=256)
    halo_s = jnp.pad(st['conv'], ((0, 0), (CONV_HALO - (CONV_WIDTH - 1), 0), (0, 0)))
    oa_s, cst_s = conv_mixer(one_tile_per_batch(zs[:, :C_U]), halo_s, *conv_args, nb=DEC_BATCH, t_len=8, tt=8, rb=8, n_last=1)
    new['conv_p'] = cst_p[:, CONV_HALO - (CONV_WIDTH - 1):]
    new['conv_s'] = cst_s[:, CONV_HALO - (CONV_WIDTH - 1):]
    o_a = slab(oa_p, oa_s[::8])

    ws = wts['sgu_ws']
    w_causal = (ws * jnp.tril(jnp.ones((SGU_CHUNK, SGU_CHUNK), F32))).astype(BF16)
    bias_rows = jnp.repeat(wts['sgu_bs'].T, LANES, axis=1)
    ob_p = sgu_mixer(z1, w_causal, bias_rows, wts['sgu_ln_g'], wts['sgu_ln_b'], rows=M_PROMPT)
    w0 = jnp.repeat(ws[:, 0, 0], LANES)[None, :]
    ob_s, v_s = sgu_step(zs, w0, bias_rows[0:1], wts['sgu_ln_g'], wts['sgu_ln_b'])
    new['sgu_v'] = v_s[:, None, :]
    o_b = slab(ob_p, ob_s)

    w_a2p = jnp.zeros((LANES, GLA_KEYS), F32).at[:GLA_RANK].set(wts['gla_w_a2']).astype(BF16)
    gla_args = (w_a2p, wts['gla_b_a'], wts['gla_norm'])
    s0_p = jnp.zeros((BATCH, GLA_PAIRS, GLA_DV, LANES), F32)
    od_p, sp = gla_mixer(z2, z3, s0_p, *gla_args, nb=BATCH, t_len=SEQ, tt=256, chunk=GLA_CHUNK, n_valid=GLA_CHUNK)
    od_s, ss = gla_mixer(one_tile_per_batch(z2[sample]), one_tile_per_batch(z3[sample]), gla_state_in(st['gla']), *gla_args,
                         nb=DEC_BATCH, t_len=8, tt=8, chunk=8, n_valid=1)
    new['gla_p'] = gla_state_out(sp)
    new['gla_s'] = gla_state_out(ss)
    o_d = slab(od_p, od_s[::8])

    w_cmp = wts['nsa_w_cmp'].astype(BF16)
    cb_p = compress_prompt(z1, w_cmp)
    oc_p = nsa_prompt(z1, cb_p, tabs['bias_c'], tabs['t0'], tabs['t1'], tabs['t2'], tabs['cover_t'], tabs['expand'])
    pt = st['page_table']
    cb_s = compress_paged(_cache_rows(st['cmp']), pt, w_cmp, layer)
    q_s = zs[:, C_Q:C_Q + NSA_HEADS * LANES].reshape(DEC_BATCH, NSA_HEADS, LANES)
    kvw_new = zs[:, None, C_WIN:C_WIN + 4 * LANES]
    kvs_new = zs[:, None, C_SLC:C_SLC + 4 * LANES]
    win_buf = st['win'].reshape(DEC_BATCH, WINDOW, 4 * LANES)
    ocw, sel = nsa_step_a(q_s, cb_s, tabs['bias_cs'], tabs['cover_s'], win_buf, kvw_new, tabs['bias_w'], tabs['bias_0'])
    n_steps = N_PAGES // PAGES_PER_STEP
    blocks_per_step = CMP_ROWS // SLC_LEN
    sel_kv = jnp.repeat(sel[:, :KVH], GRP, axis=1)
    sel_r = sel_kv[:, :, :n_steps * blocks_per_step].reshape(DEC_BATCH, NSA_HEADS, n_steps, blocks_per_step)
    sel_r = jnp.pad(jnp.swapaxes(sel_r, 1, 2), ((0, 0), (0, 0), (0, 0), (0, LANES - blocks_per_step)))
    sel_new = jnp.broadcast_to(sel_kv[:, :, PAST_LEN // SLC_LEN, None], (DEC_BATCH, NSA_HEADS, LANES))
    gates_s = zs[:, C_CG:C_CG + 3 * NSA_HEADS].reshape(DEC_BATCH, NSA_HEADS, 3)
    gates_s = jnp.pad(gates_s, ((0, 0), (0, 0), (0, LANES - 3)))
    oc_s = nsa_step_b(_cache_rows(st['slc']), pt, layer, q_s, sel_r, tabs['expand'], tabs['bias_s'], kvs_new, sel_new, tabs['bias_0'], ocw, gates_s)
    o_c = slab(oc_p, oc_s.reshape(DEC_BATCH, W_MIX))

    cmp_p, slc_p, win_p = kv_export(z1)
    new['cmp_p'] = cmp_p.reshape(BATCH, SEQ // PAGE_SIZE, PAGE_SIZE, 2, KVH, HEAD_DIM)
    new['slc_p'] = slc_p.reshape(BATCH, SEQ // PAGE_SIZE, PAGE_SIZE, 2, KVH, HEAD_DIM)
    new['win_p'] = win_p.reshape(BATCH, WINDOW, 2, KVH, HEAD_DIM)
    kvs = lambda c: zs[:, c:c + 4 * LANES].reshape(DEC_BATCH, 1, 2, KVH, HEAD_DIM)
    new['cmp_s'] = kvs(C_CMP)
    new['slc_s'] = kvs(C_SLC)
    new['win_s'] = jnp.concatenate([st['win'][:, 1:], kvs(C_WIN)], axis=1)
    return (o_a, o_b, o_c, o_d), new


def kernel(x_prompt, x_sample, cache_cmp_kv, cache_slc_kv, state_win_kv, state_conv, state_gla, page_table, p_prompt, p_sample, ffn1_norm, ffn1_w_in, ffn1_w_out, mix_norm, w_in, conv_w, conv_b, conv_ln_g, conv_ln_b, conv_w_out, sgu_ln_g, sgu_ln_b, sgu_ws, sgu_bs, sgu_w_out, nsa_w_cmp, nsa_w_out, gla_w_a2, gla_b_a, gla_norm, gla_w_out, w_out, ffn2_norm, ffn2_w_in, ffn2_w_out, pe_norm, w_pe, w_pe_gate, rel_bias, final_norm):
    n_tail = M_SLAB - M_PROMPT - DEC_BATCH
    x = jnp.concatenate([x_prompt.reshape(M_PROMPT, D_MODEL), x_sample.reshape(DEC_BATCH, D_MODEL),
                         jnp.zeros((n_tail, D_MODEL), F32)], axis=0)
    tabs = _nsa_tables(rel_bias)
    w_in_t = jnp.swapaxes(w_in, 1, 2)
    news = []
    for i in range(DEPTH):
        _, h = add_norm(x, None, ffn1_norm[i], write_x=False)
        delta = ffn_half(h, ffn1_w_in, ffn1_w_out, i)
        x, h = add_norm(x, delta, mix_norm[i])
        z1 = matmul_window(h, w_in_t, i, 0, Z1W)
        z2 = matmul_window(h, w_in_t, i, Z2_OFF, Z2W)
        z3 = matmul_window(h, w_in_t, i, Z3_OFF, Z3W)
        gates = matmul_window(h, w_in_t, i, G_OFF, N_BRANCH * D_MODEL, act="sigmoid", out_dtype=BF16)
        wts = dict(conv_w=conv_w[i], conv_b=conv_b[i], conv_ln_g=conv_ln_g[i], conv_ln_b=conv_ln_b[i],
                   sgu_ws=sgu_ws[i], sgu_bs=sgu_bs[i], sgu_ln_g=sgu_ln_g[i], sgu_ln_b=sgu_ln_b[i],
                   nsa_w_cmp=nsa_w_cmp[i], gla_w_a2=gla_w_a2[i], gla_b_a=gla_b_a[i], gla_norm=gla_norm[i])
        st = dict(cmp=cache_cmp_kv, slc=cache_slc_kv, win=state_win_kv[i], conv=state_conv[i], gla=state_gla[i],
                  page_table=page_table)
        branches, new = _mixers(i, z1, z2, z3, tabs, wts, st)
        news.append(new)
        merged = merge(branches, (conv_w_out, sgu_w_out, nsa_w_out, gla_w_out), gates, i)
        x = matmul(merged, w_out, tn=512, layer=i, residual=x)
        _, h = add_norm(x, None, ffn2_norm[i], write_x=False)
        delta = ffn_half(h, ffn2_w_in, ffn2_w_out, i)
        x, h = add_norm(x, delta, pe_norm[i])
        ple = jnp.concatenate([p_prompt[i].reshape(M_PROMPT, PLE_DIM), p_sample[i].reshape(DEC_BATCH, PLE_DIM),
                               jnp.zeros((n_tail, PLE_DIM), F32)], axis=0).astype(BF16)
        x = ple_delta(h, ple, w_pe_gate, w_pe, i, x)
    y_prompt = final_norm_rows(x, final_norm, first_row=0, n_rows=M_PROMPT, tile=256)
    y_sample = final_norm_rows(x, final_norm, first_row=M_PROMPT, n_rows=16, tile=16)[:DEC_BATCH]
    stack = lambda name: jnp.stack([n[name] for n in news], axis=0)
    return (y_prompt.reshape(BATCH, SEQ, D_MODEL), y_sample.reshape(DEC_BATCH, 1, D_MODEL),
            stack('cmp_p'), stack('cmp_s'), stack('slc_p'), stack('slc_s'), stack('win_p'), stack('win_s'),
            stack('conv_p'), stack('conv_s'), stack('gla_p'), stack('gla_s'), stack('sgu_v'))
```
